```python
import jax, jax.numpy as jnp
from jax import lax
import numpy as np

D_MODEL = 2048
BATCH = 2
SEQ = 4096
DEPTH = 4

PLE_DIM = 256
EPS = 1e-6
CONV_CH = D_MODEL // 4
CONV_WIDTH = 31
GLA_HEADS = 4
GLA_DV = D_MODEL // 16
GLA_DK = GLA_DV // 2
GLA_GATE_RANK = 16
GLA_TAU = 16.0
GLA_CHUNK = 64
MOBA_HEADS = 8
MOBA_HD = D_MODEL // 16
MOBA_BLOCK = 256
MOBA_TOPK = 3
MOBA_Q_CHUNK = 64
N_BRANCH = 3
FFN_DIM = ((8 * D_MODEL // 3 + 255) // 256) * 256
FFN_CONV_WIDTH = 3
IN_SIZES = (2 * CONV_CH,
            GLA_HEADS * GLA_DK,
            GLA_HEADS * GLA_DK,
            GLA_HEADS * GLA_DV,
            GLA_HEADS * GLA_DV,
            GLA_GATE_RANK,
            MOBA_HEADS * MOBA_HD,
            MOBA_HEADS * MOBA_HD,
            MOBA_HEADS * MOBA_HD)
IN_COLS = sum(IN_SIZES)

kernel_name = "hybrid_conformer_gla_moba_block"


def rms_norm(x, g):
    xf = x.astype(jnp.float32)
    y = xf * lax.rsqrt(jnp.mean(xf * xf, axis=-1, keepdims=True) + EPS)
    return (y * g.astype(jnp.float32)).astype(x.dtype)


def layer_norm(x, g, b):
    xf = x.astype(jnp.float32)
    mu = jnp.mean(xf, axis=-1, keepdims=True)
    xc = xf - mu
    y = xc * lax.rsqrt(jnp.mean(xc * xc, axis=-1, keepdims=True) + EPS)
    return (y * g.astype(jnp.float32) + b.astype(jnp.float32)).astype(x.dtype)


def causal_dwconv(x, w):
    k_w = w.shape[0]
    s = x.shape[1]
    xp = jnp.pad(x, ((0, 0), (k_w - 1, 0), (0, 0)))
    out = xp[:, k_w - 1:k_w - 1 + s] * w[k_w - 1]
    for j in range(k_w - 1):
        out = out + xp[:, j:j + s] * w[j]
    return out


def conformer_conv_branch(z_glu, conv_w, conv_b, ln_g, ln_b, w_o):
    val, gate = jnp.split(z_glu, 2, axis=-1)
    a = val * jax.nn.sigmoid(gate)
    a = causal_dwconv(a, conv_w) + conv_b
    a = jax.nn.silu(layer_norm(a, ln_g, ln_b))
    return a @ w_o


def gla_branch(q, k, v, r, a_low, w_a2, b_a, norm_g, w_o):
    b_sz, s, _ = q.shape
    n_c = s // GLA_CHUNK
    q = q.reshape(b_sz, s, GLA_HEADS, GLA_DK) * (GLA_DK ** -0.5)
    k = k.reshape(b_sz, s, GLA_HEADS, GLA_DK)
    v = v.reshape(b_sz, s, GLA_HEADS, GLA_DV)
    log_a = jax.nn.log_sigmoid((a_low @ w_a2 + b_a).astype(jnp.float32)) / GLA_TAU
    log_a = log_a.reshape(b_sz, s, GLA_HEADS, GLA_DK)

    def to_chunks(t):
        return t.reshape(b_sz, n_c, GLA_CHUNK, GLA_HEADS, -1).transpose(1, 0, 3, 2, 4).astype(jnp.float32)

    tri = jnp.tril(jnp.ones((GLA_CHUNK, GLA_CHUNK), dtype=bool))

    def step(state, xs):
        qc, kc, vc, lac = xs
        bc = jnp.cumsum(lac, axis=2)
        inter = jnp.einsum('bhtd,bhde->bhte', qc * jnp.exp(bc), state)
        diff = bc[:, :, :, None, :] - bc[:, :, None, :, :]
        decay = jnp.exp(jnp.where(tri[None, None, :, :, None], diff, -jnp.inf))
        attn = jnp.einsum('bhtd,bhsd,bhtsd->bhts', qc, kc, decay)
        intra = jnp.einsum('bhts,bhse->bhte', attn, vc)
        b_last = bc[:, :, -1, :]
        k_dec = kc * jnp.exp(b_last[:, :, None, :] - bc)
        state = state * jnp.exp(b_last)[..., None] + jnp.einsum('bhsd,bhse->bhde', k_dec, vc)
        return state, inter + intra

    state0 = jnp.zeros((b_sz, GLA_HEADS, GLA_DK, GLA_DV), jnp.float32)
    _, o = lax.scan(step, state0, (to_chunks(q), to_chunks(k), to_chunks(v), to_chunks(log_a)))
    o = o.transpose(1, 0, 3, 2, 4).reshape(b_sz, s, GLA_HEADS, GLA_DV).astype(v.dtype)
    o = rms_norm(o, norm_g) * jax.nn.silu(r.reshape(b_sz, s, GLA_HEADS, GLA_DV))
    return o.reshape(b_sz, s, GLA_HEADS * GLA_DV) @ w_o


def moba_branch(q, k, v, q_norm_g, k_norm_g, w_o):
    b_sz, s, _ = q.shape
    q = rms_norm(q.reshape(b_sz, s, MOBA_HEADS, MOBA_HD), q_norm_g).transpose(0, 2, 1, 3)
    k = rms_norm(k.reshape(b_sz, s, MOBA_HEADS, MOBA_HD), k_norm_g).transpose(0, 2, 1, 3)
    v = v.reshape(b_sz, s, MOBA_HEADS, MOBA_HD).transpose(0, 2, 1, 3)
    s_pad = -(-s // MOBA_BLOCK) * MOBA_BLOCK
    pad = ((0, 0), (0, 0), (0, s_pad - s), (0, 0))
    q, k, v = jnp.pad(q, pad), jnp.pad(k, pad), jnp.pad(v, pad)
    n_blk = s_pad // MOBA_BLOCK
    top_k = min(MOBA_TOPK, n_blk)
    kb = k.reshape(b_sz, MOBA_HEADS, n_blk, MOBA_BLOCK, MOBA_HD)
    vb = v.reshape(b_sz, MOBA_HEADS, n_blk, MOBA_BLOCK, MOBA_HD)
    k_mean = jnp.mean(kb.astype(jnp.float32), axis=3)
    n_q = s_pad // MOBA_Q_CHUNK
    q_chunks = q.reshape(b_sz, MOBA_HEADS, n_q, MOBA_Q_CHUNK, MOBA_HD).transpose(2, 0, 1, 3, 4)
    b_idx = jnp.arange(b_sz)[:, None, None, None]
    h_idx = jnp.arange(MOBA_HEADS)[None, :, None, None]
    scale = MOBA_HD ** -0.5

    def chunk(args):
        c, qc = args
        q_pos = c * MOBA_Q_CHUNK + jnp.arange(MOBA_Q_CHUNK)
        q_blk = q_pos // MOBA_BLOCK
        blk_s = jnp.einsum('bhqd,bhnd->bhqn', qc.astype(jnp.float32), k_mean)
        past = jnp.arange(n_blk)[None, :] < q_blk[:, None]
        blk_s = jnp.where(past, blk_s, -jnp.inf)
        top_val, top_idx = lax.top_k(blk_s, top_k)
        sel_ok = jnp.isfinite(top_val)
        k_sel = kb[b_idx, h_idx, top_idx]
        v_sel = vb[b_idx, h_idx, top_idx]
        s_sel = jnp.einsum('bhqd,bhqjkd->bhqjk', qc, k_sel).astype(jnp.float32) * scale
        s_sel = jnp.where(sel_ok[..., None], s_sel, -jnp.inf)
        s_sel = s_sel.reshape(b_sz, MOBA_HEADS, MOBA_Q_CHUNK, top_k * MOBA_BLOCK)
        own_start = (c * MOBA_Q_CHUNK // MOBA_BLOCK) * MOBA_BLOCK
        k_own = lax.dynamic_slice_in_dim(k, own_start, MOBA_BLOCK, axis=2)
        v_own = lax.dynamic_slice_in_dim(v, own_start, MOBA_BLOCK, axis=2)
        s_own = jnp.einsum('bhqd,bhkd->bhqk', qc, k_own).astype(jnp.float32) * scale
        k_pos = own_start + jnp.arange(MOBA_BLOCK)
        s_own = jnp.where(k_pos[None, :] <= q_pos[:, None], s_own, -jnp.inf)
        probs = jax.nn.softmax(jnp.concatenate([s_sel, s_own], axis=-1), axis=-1).astype(v.dtype)
        p_sel = probs[..., :top_k * MOBA_BLOCK].reshape(b_sz, MOBA_HEADS, MOBA_Q_CHUNK, top_k, MOBA_BLOCK)
        p_own = probs[..., top_k * MOBA_BLOCK:]
        return (jnp.einsum('bhqjk,bhqjkd->bhqd', p_sel, v_sel)
                + jnp.einsum('bhqk,bhkd->bhqd', p_own, v_own))

    o = lax.map(chunk, (jnp.arange(n_q), q_chunks))
    o = o.transpose(1, 0, 3, 2, 4).reshape(b_sz, s_pad, MOBA_HEADS * MOBA_HD)[:, :s]
    return o @ w_o


def setup_inputs(seed: int = 0) -> dict:
    key = jax.random.key(seed)
    ks = jax.random.split(key, 32)
    L, D, F = DEPTH, D_MODEL, FFN_DIM

    def w(k, shape, fan_in):
        return jax.random.normal(k, shape, jnp.float32) * (fan_in ** -0.5)

    def gain(k, shape):
        return 1.0 + 0.05 * jax.random.normal(k, shape, jnp.float32)

    def bias(k, shape):
        return 0.01 * jax.random.normal(k, shape, jnp.float32)

    return {
        "x": jax.random.normal(ks[0], (BATCH, SEQ, D), jnp.float32),
        "p": jax.random.normal(ks[1], (DEPTH, BATCH, SEQ, PLE_DIM), jnp.float32),
        "norm_mix_g": gain(ks[2], (L, D)),
        "w_in": w(ks[3], (L, D, IN_COLS), D),
        "conv_w": w(ks[4], (L, CONV_WIDTH, CONV_CH), CONV_WIDTH),
        "conv_b": bias(ks[5], (L, CONV_CH)),
        "conv_ln_g": gain(ks[6], (L, CONV_CH)),
        "conv_ln_b": bias(ks[7], (L, CONV_CH)),
        "w_conv_out": w(ks[8], (L, CONV_CH, D), CONV_CH),
        "gla_w_a2": w(ks[9], (L, GLA_GATE_RANK, GLA_HEADS * GLA_DK), GLA_GATE_RANK),
        "gla_b_a": bias(ks[10], (L, GLA_HEADS * GLA_DK)),
        "gla_norm_g": gain(ks[11], (L, GLA_DV)),
        "w_gla_out": w(ks[12], (L, GLA_HEADS * GLA_DV, D), GLA_HEADS * GLA_DV),
        "moba_q_norm_g": gain(ks[13], (L, MOBA_HD)),
        "moba_k_norm_g": gain(ks[14], (L, MOBA_HD)),
        "w_moba_out": w(ks[15], (L, MOBA_HEADS * MOBA_HD, D), MOBA_HEADS * MOBA_HD),
        "w_gate": w(ks[16], (L, D, N_BRANCH * D), D),
        "b_gate": bias(ks[17], (L, N_BRANCH * D)),
        "w_out": w(ks[18], (L, D, D), D),
        "norm_ffn_g": gain(ks[19], (L, D)),
        "w_up": w(ks[20], (L, D, 2 * F), D),
        "ffn_conv_w": w(ks[21], (L, FFN_CONV_WIDTH, 2 * F), FFN_CONV_WIDTH),
        "w_down": w(ks[22], (L, F, D), F),
        "norm_ple_g": gain(ks[23], (L, D)),
        "w_ple_gate": w(ks[24], (L, D, D), D),
        "w_ple": w(ks[25], (L, PLE_DIM, D), PLE_DIM),
    }


def reference(x, p, norm_mix_g, w_in, conv_w, conv_b, conv_ln_g, conv_ln_b, w_conv_out,
              gla_w_a2, gla_b_a, gla_norm_g, w_gla_out, moba_q_norm_g, moba_k_norm_g, w_moba_out,
              w_gate, b_gate, w_out, norm_ffn_g, w_up, ffn_conv_w, w_down,
              norm_ple_g, w_ple_gate, w_ple):
    b_sz, s, d = x.shape
    offsets = np.cumsum(IN_SIZES)[:-1].tolist()
    for i in range(DEPTH):
        h = rms_norm(x, norm_mix_g[i])
        z = h @ w_in[i]
        z_glu, gq, gk, gv, gr, ga, mq, mk, mv = jnp.split(z, offsets, axis=-1)
        y_a = conformer_conv_branch(z_glu, conv_w[i], conv_b[i], conv_ln_g[i], conv_ln_b[i], w_conv_out[i])
        y_b = gla_branch(gq, gk, gv, gr, ga, gla_w_a2[i], gla_b_a[i], gla_norm_g[i], w_gla_out[i])
        y_c = moba_branch(mq, mk, mv, moba_q_norm_g[i], moba_k_norm_g[i], w_moba_out[i])
        gates = jax.nn.sigmoid(h @ w_gate[i] + b_gate[i]).reshape(b_sz, s, N_BRANCH, d)
        merged = gates[:, :, 0] * y_a + gates[:, :, 1] * y_b + gates[:, :, 2] * y_c
        x = x + merged @ w_out[i]
        h = rms_norm(x, norm_ffn_g[i])
        u = causal_dwconv(h @ w_up[i], ffn_conv_w[i])
        u_g, u_v = jnp.split(u, 2, axis=-1)
        x = x + (jax.nn.silu(u_g) * u_v) @ w_down[i]
        g = jax.nn.sigmoid(rms_norm(x, norm_ple_g[i]) @ w_ple_gate[i])
        x = x + g * (p[i] @ w_ple[i])
    return x
```

```python
import functools

import jax
import jax.numpy as jnp
from jax import lax
from jax.experimental import pallas as pl
from jax.experimental.pallas import tpu as pltpu

F32 = jnp.float32
BF16 = jnp.bfloat16

D_MODEL = 2048
PLE_DIM = 256
EPS = 1e-6
CONV_CH = 512
CONV_WIDTH = 31
GLA_HEADS = 4
GLA_DV = 128
GLA_DK = 64
GLA_GATE_RANK = 16
GLA_TAU = 16.0
GLA_CHUNK = 64
GLA_SUB = 16
MOBA_HEADS = 8
MOBA_HD = 128
MOBA_BLOCK = 256
MOBA_TOPK = 3
FFN_DIM = 5632
FFN_CONV_WIDTH = 3

LANE = 128
GATE_PAD = LANE
OFF_GLU = 0
OFF_GQ = OFF_GLU + 2 * CONV_CH
OFF_GK = OFF_GQ + GLA_HEADS * GLA_DK
OFF_GV = OFF_GK + GLA_HEADS * GLA_DK
OFF_GR = OFF_GV + GLA_HEADS * GLA_DV
OFF_GA = OFF_GR + GLA_HEADS * GLA_DV
OFF_MQ = OFF_GA + GATE_PAD
OFF_MK = OFF_MQ + MOBA_HEADS * MOBA_HD
OFF_MV = OFF_MK + MOBA_HEADS * MOBA_HD
Z_COLS = OFF_MV + MOBA_HEADS * MOBA_HD

VMEM_LIMIT = 56 * 1024 * 1024


def _cparams(sem):
    return pltpu.CompilerParams(dimension_semantics=sem, vmem_limit_bytes=VMEM_LIMIT)


def _sigmoid(x):
    return 1.0 / (1.0 + jnp.exp(-x))


def _silu(x):
    return x * _sigmoid(x)


def _rms_rows(x, g):
    ms = jnp.mean(x * x, axis=-1, keepdims=True)
    return x * lax.rsqrt(ms + EPS) * g


def _norm_matmul_kernel(x_ref, g_ref, w_ref, b_ref, o_ref, h_scr, *, sigmoid):
    @pl.when(pl.program_id(1) == 0)
    def _():
        h_scr[...] = _rms_rows(x_ref[...], g_ref[...]).astype(BF16)

    acc = jnp.dot(h_scr[...], w_ref[...], preferred_element_type=F32)
    if sigmoid:
        acc = _sigmoid(acc + b_ref[...])
    o_ref[...] = acc.astype(o_ref.dtype)


def _norm_matmul(x, g, w, b, *, sigmoid, tm, tn, name):
    m, d = x.shape
    n = w.shape[1]
    return pl.pallas_call(
        functools.partial(_norm_matmul_kernel, sigmoid=sigmoid),
        out_shape=jax.ShapeDtypeStruct((m, n), BF16),
        grid=(m // tm, n // tn),
        in_specs=[
            pl.BlockSpec((tm, d), lambda i, j: (i, 0)),
            pl.BlockSpec((1, d), lambda i, j: (0, 0)),
            pl.BlockSpec((d, tn), lambda i, j: (0, j)),
            pl.BlockSpec((1, tn), lambda i, j: (0, j)),
        ],
        out_specs=pl.BlockSpec((tm, tn), lambda i, j: (i, j)),
        scratch_shapes=[pltpu.VMEM((tm, d), BF16)],
        compiler_params=_cparams(("arbitrary", "arbitrary")),
        name=name,
    )(x, g, w, b)


CONV_HALO = 32
CONV_ROWS = 64


def _conv_kernel(z_ref, w_ref, cb_ref, lg_ref, lb_ref, o_ref, abuf, *, ts):
    t = pl.program_id(1)

    @pl.when(t == 0)
    def _():
        abuf[0:CONV_HALO, :] = jnp.zeros((CONV_HALO, CONV_CH), F32)

    @pl.when(t > 0)
    def _():
        abuf[0:CONV_HALO, :] = abuf[ts:ts + CONV_HALO, :]

    z = z_ref[...].astype(F32)
    abuf[CONV_HALO:CONV_HALO + ts, :] = z[:, :CONV_CH] * _sigmoid(z[:, CONV_CH:])

    first = CONV_HALO - (CONV_WIDTH - 1)
    for r0 in range(0, ts, CONV_ROWS):
        acc = abuf[r0 + CONV_HALO:r0 + CONV_HALO + CONV_ROWS, :] * w_ref[CONV_WIDTH - 1:CONV_WIDTH, :]
        for j in range(CONV_WIDTH - 1):
            acc = acc + abuf[r0 + first + j:r0 + first + j + CONV_ROWS, :] * w_ref[j:j + 1, :]
        acc = acc + cb_ref[...]
        mu = jnp.mean(acc, axis=-1, keepdims=True)
        xc = acc - mu
        var = jnp.mean(xc * xc, axis=-1, keepdims=True)
        y = xc * lax.rsqrt(var + EPS) * lg_ref[...] + lb_ref[...]
        o_ref[r0:r0 + CONV_ROWS, :] = _silu(y).astype(o_ref.dtype)


def _conformer_conv(z3, conv_w, conv_b, ln_g, ln_b, *, ts=512):
    b_sz, s, _ = z3.shape
    vec = pl.BlockSpec((1, CONV_CH), lambda b, t: (0, 0))
    return pl.pallas_call(
        functools.partial(_conv_kernel, ts=ts),
        out_shape=jax.ShapeDtypeStruct((b_sz, s, CONV_CH), BF16),
        grid=(b_sz, s // ts),
        in_specs=[
            pl.BlockSpec((None, ts, 2 * CONV_CH), lambda b, t: (b, t, OFF_GLU // (2 * CONV_CH))),
            pl.BlockSpec((CONV_WIDTH, CONV_CH), lambda b, t: (0, 0)),
            vec, vec, vec,
        ],
        out_specs=pl.BlockSpec((None, ts, CONV_CH), lambda b, t: (b, t, 0)),
        scratch_shapes=[pltpu.VMEM((CONV_HALO + ts, CONV_CH), F32)],
        compiler_params=_cparams(("arbitrary", "arbitrary")),
        name="conformer_conv",
    )(z3, conv_w, conv_b, ln_g, ln_b)


GLA_QK = GLA_HEADS * GLA_DK
GLA_V = GLA_HEADS * GLA_DV


def _gla_kernel(q_ref, k_ref, v_ref, r_ref, al_ref, wa_ref, ba_ref, ng_ref, o_ref,
                st, qf, kf, vf, bcs, pbig, intra):
    c = GLA_CHUNK

    @pl.when(pl.program_id(1) == 0)
    def _():
        st[...] = jnp.zeros(st.shape, F32)

    xa = jnp.dot(al_ref[...], wa_ref[...], preferred_element_type=F32) + ba_ref[...]
    la = (jnp.minimum(xa, 0.0) - jnp.log(1.0 + jnp.exp(-jnp.abs(xa)))) * (1.0 / GLA_TAU)
    row = lax.broadcasted_iota(jnp.int32, (c, c), 0)
    col = lax.broadcasted_iota(jnp.int32, (c, c), 1)
    tri = jnp.where(row >= col, 1.0, 0.0).astype(BF16)
    la_hi = la.astype(BF16)
    la_lo = (la - la_hi.astype(F32)).astype(BF16)
    bc = (jnp.dot(tri, la_hi, preferred_element_type=F32)
          + jnp.dot(tri, la_lo, preferred_element_type=F32))

    q = q_ref[...].astype(F32) * (GLA_DK ** -0.5)
    k = k_ref[...].astype(F32)
    qf[...] = q
    kf[...] = k
    vf[...] = v_ref[...].astype(F32)
    bcs[...] = bc

    b_last = bc[c - 1:c, :]
    q_in = (q * jnp.exp(bc)).astype(BF16)
    k_dec = (k * jnp.exp(b_last - bc)).astype(BF16)
    e_last = jnp.exp(b_last)
    v_bf = v_ref[...]

    srow = lax.broadcasted_iota(jnp.int32, (GLA_QK, GLA_V), 0) // GLA_DK
    scol = lax.broadcasted_iota(jnp.int32, (GLA_QK, GLA_V), 1) // GLA_DV
    sel = jnp.where(srow == scol, 1.0, 0.0).astype(BF16)

    for j in range(c // GLA_SUB):
        t0 = j * GLA_SUB
        tj = c - t0
        q_rows = qf[t0:c, :]
        bc_rows = bcs[t0:c, :]
        for sl in range(GLA_SUB):
            s = t0 + sl
            e = jnp.exp(jnp.minimum(bc_rows - bcs[s:s + 1, :], 0.0))
            pbig[sl * tj:(sl + 1) * tj, :] = ((q_rows * kf[s:s + 1, :]) * e).astype(BF16)
        r_all = jnp.dot(pbig[0:GLA_SUB * tj, :], sel, preferred_element_type=F32)
        t_loc = lax.broadcasted_iota(jnp.int32, (tj, GLA_V), 0)
        acc = jnp.zeros((tj, GLA_V), F32)
        for sl in range(GLA_SUB):
            s = t0 + sl
            blk = r_all[sl * tj:(sl + 1) * tj, :]
            blk = jnp.where(t_loc >= sl, blk, 0.0)
            acc = acc + blk * vf[s:s + 1, :]
        if j == 0:
            intra[...] = acc
        else:
            intra[t0:c, :] = intra[t0:c, :] + acc

    for h in range(GLA_HEADS):
        ks = slice(h * GLA_DK, (h + 1) * GLA_DK)
        vs = slice(h * GLA_DV, (h + 1) * GLA_DV)
        st_h = st[h]
        inter = lax.dot_general(q_in[:, ks], st_h.astype(BF16), (((1,), (1,)), ((), ())),
                                preferred_element_type=F32)
        upd = lax.dot_general(v_bf[:, vs], k_dec[:, ks], (((0,), (0,)), ((), ())),
                              preferred_element_type=F32)
        st[h] = st_h * e_last[:, ks] + upd
        o_h = inter + intra[:, vs]
        y = _rms_rows(o_h, ng_ref[...])
        o_ref[:, vs] = (y * _silu(r_ref[:, vs].astype(F32))).astype(o_ref.dtype)


def _gla(z3, w_a2p, b_a, norm_g):
    b_sz, s, _ = z3.shape
    c = GLA_CHUNK

    def zspec(width, off):
        return pl.BlockSpec((None, c, width), lambda b, t: (b, t, off // width))

    return pl.pallas_call(
        _gla_kernel,
        out_shape=jax.ShapeDtypeStruct((b_sz, s, GLA_V), BF16),
        grid=(b_sz, s // c),
        in_specs=[
            zspec(GLA_QK, OFF_GQ), zspec(GLA_QK, OFF_GK), zspec(GLA_V, OFF_GV), zspec(GLA_V, OFF_GR),
            zspec(GATE_PAD, OFF_GA),
            pl.BlockSpec((GATE_PAD, GLA_QK), lambda b, t: (0, 0)),
            pl.BlockSpec((1, GLA_QK), lambda b, t: (0, 0)),
            pl.BlockSpec((1, GLA_DV), lambda b, t: (0, 0)),
        ],
        out_specs=pl.BlockSpec((None, c, GLA_V), lambda b, t: (b, t, 0)),
        scratch_shapes=[
            pltpu.VMEM((GLA_HEADS, GLA_DV, GLA_DK), F32),
            pltpu.VMEM((c, GLA_QK), F32),
            pltpu.VMEM((c, GLA_QK), F32),
            pltpu.VMEM((c, GLA_V), F32),
            pltpu.VMEM((c, GLA_QK), F32),
            pltpu.VMEM((GLA_SUB * c, GLA_QK), BF16),
            pltpu.VMEM((c, GLA_V), F32),
        ],
        compiler_params=_cparams(("arbitrary", "arbitrary")),
        name="gla",
    )(z3, z3, z3, z3, z3, w_a2p, b_a, norm_g)


NEG_INF = float("-inf")


def _moba_kernel(q_ref, k_ref, v_ref, gq_ref, gk_ref, o_ref, kn, kmean, selb, m_scr, l_scr, acc_scr,
                 *, n_blk):
    qi = pl.program_id(2)
    blk = MOBA_BLOCK

    @pl.when(qi == 0)
    def _():
        for n in range(n_blk):
            kb = _rms_rows(k_ref[n * blk:(n + 1) * blk, :].astype(F32), gk_ref[...])
            kn[n * blk:(n + 1) * blk, :] = kb.astype(BF16)
            kmean[n:n + 1, :] = jnp.mean(kb, axis=0, keepdims=True)

    qn = _rms_rows(q_ref[...].astype(F32), gq_ref[...])
    q_bf = (qn * (MOBA_HD ** -0.5)).astype(BF16)

    sb = lax.dot_general(qn, kmean[...], (((1,), (1,)), ((), ())),
                         precision=lax.Precision.HIGHEST, preferred_element_type=F32)
    lane = lax.broadcasted_iota(jnp.int32, (blk, n_blk), 1)
    past = lane < qi
    sb = jnp.where(past, sb, NEG_INF)
    rank = jnp.zeros((blk, n_blk), F32)
    for m in range(n_blk):
        cm = sb[:, m:m + 1]
        ge = jnp.where(cm >= sb, 1.0, 0.0)
        gt = jnp.where(cm > sb, 1.0, 0.0)
        rank = rank + jnp.where(lane > m, ge, gt)
    chosen = jnp.where(past, jnp.where(rank < float(MOBA_TOPK), 1.0, 0.0), 0.0)
    for n in range(n_blk):
        selb[n] = jnp.broadcast_to(chosen[:, n:n + 1], (blk, LANE))

    own = pl.multiple_of(qi * blk, blk)
    s = lax.dot_general(q_bf, kn[pl.ds(own, blk), :], (((1,), (1,)), ((), ())),
                        preferred_element_type=F32)
    r_i = lax.broadcasted_iota(jnp.int32, (blk, blk), 0)
    c_i = lax.broadcasted_iota(jnp.int32, (blk, blk), 1)
    s = jnp.where(c_i <= r_i, s, NEG_INF)
    m0 = jnp.broadcast_to(jnp.max(s, axis=-1, keepdims=True), (blk, LANE))
    p = jnp.concatenate([jnp.exp(s[:, :LANE] - m0), jnp.exp(s[:, LANE:] - m0)], axis=1)
    m_scr[...] = m0
    l_scr[...] = jnp.broadcast_to(jnp.sum(p, axis=-1, keepdims=True), (blk, LANE))
    acc_scr[...] = jnp.dot(p.astype(BF16), v_ref[pl.ds(own, blk), :], preferred_element_type=F32)

    def past_block(n, carry):
        start = pl.multiple_of(n * blk, blk)
        sn = lax.dot_general(q_bf, kn[pl.ds(start, blk), :], (((1,), (1,)), ((), ())),
                             preferred_element_type=F32)
        on = selb[n] > 0.0
        s0 = jnp.where(on, sn[:, :LANE], NEG_INF)
        s1 = jnp.where(on, sn[:, LANE:], NEG_INF)
        m_prev = m_scr[...]
        m_cur = jnp.max(jnp.maximum(s0, s1), axis=-1, keepdims=True)
        m_new = jnp.maximum(m_prev, jnp.broadcast_to(m_cur, (blk, LANE)))
        alpha = jnp.exp(m_prev - m_new)
        pn = jnp.concatenate([jnp.exp(s0 - m_new), jnp.exp(s1 - m_new)], axis=1)
        l_scr[...] = alpha * l_scr[...] + jnp.broadcast_to(jnp.sum(pn, axis=-1, keepdims=True), (blk, LANE))
        acc_scr[...] = alpha * acc_scr[...] + jnp.dot(pn.astype(BF16), v_ref[pl.ds(start, blk), :],
                                                      preferred_element_type=F32)
        m_scr[...] = m_new
        return carry

    lax.fori_loop(0, qi, past_block, 0)
    o_ref[...] = (acc_scr[...] / l_scr[...]).astype(o_ref.dtype)


def _moba(z3, gq, gk):
    b_sz, s, _ = z3.shape
    n_blk = s // MOBA_BLOCK
    hd = MOBA_HD
    return pl.pallas_call(
        functools.partial(_moba_kernel, n_blk=n_blk),
        out_shape=jax.ShapeDtypeStruct((b_sz, s, MOBA_HEADS * hd), BF16),
        grid=(b_sz, MOBA_HEADS, n_blk),
        in_specs=[
            pl.BlockSpec((None, MOBA_BLOCK, hd), lambda b, h, i: (b, i, OFF_MQ // hd + h)),
            pl.BlockSpec((None, s, hd), lambda b, h, i: (b, 0, OFF_MK // hd + h)),
            pl.BlockSpec((None, s, hd), lambda b, h, i: (b, 0, OFF_MV // hd + h)),
            pl.BlockSpec((1, hd), lambda b, h, i: (0, 0)),
            pl.BlockSpec((1, hd), lambda b, h, i: (0, 0)),
        ],
        out_specs=pl.BlockSpec((None, MOBA_BLOCK, hd), lambda b, h, i: (b, i, h)),
        scratch_shapes=[
            pltpu.VMEM((s, hd), BF16),
            pltpu.VMEM((n_blk, hd), F32),
            pltpu.VMEM((n_blk, MOBA_BLOCK, LANE), F32),
            pltpu.VMEM((MOBA_BLOCK, LANE), F32),
            pltpu.VMEM((MOBA_BLOCK, LANE), F32),
            pltpu.VMEM((MOBA_BLOCK, hd), F32),
        ],
        compiler_params=_cparams(("arbitrary", "arbitrary", "arbitrary")),
        name="moba",
    )(z3, z3, z3, gq, gk)


MERGE_COLS = 512


def _merge_kernel(a_ref, gl_ref, mo_ref, gt_ref, x_ref, wc_ref, wg_ref, wm_ref, wo_ref, o_ref, mg):
    d = D_MODEL
    for c0 in range(0, d, MERGE_COLS):
        cs = slice(c0, c0 + MERGE_COLS)
        ya = jnp.dot(a_ref[...], wc_ref[:, cs], preferred_element_type=F32)
        yb = jnp.dot(gl_ref[...], wg_ref[:, cs], preferred_element_type=F32)
        yc = jnp.dot(mo_ref[...], wm_ref[:, cs], preferred_element_type=F32)
        g0 = gt_ref[:, c0:c0 + MERGE_COLS].astype(F32)
        g1 = gt_ref[:, d + c0:d + c0 + MERGE_COLS].astype(F32)
        g2 = gt_ref[:, 2 * d + c0:2 * d + c0 + MERGE_COLS].astype(F32)
        mg[:, cs] = (g0 * ya + g1 * yb + g2 * yc).astype(BF16)
    o_ref[...] = x_ref[...] + jnp.dot(mg[...], wo_ref[...], preferred_element_type=F32)


def _merge(a_act, o_gla, o_moba, gates, x, wc, wg, wm, wo, *, tm=256):
    m, d = x.shape

    def rows(width):
        return pl.BlockSpec((tm, width), lambda i: (i, 0))

    def whole(w):
        return pl.BlockSpec(w.shape, lambda i: (0, 0), pipeline_mode=pl.Buffered(1))

    return pl.pallas_call(
        _merge_kernel,
        out_shape=jax.ShapeDtypeStruct((m, d), F32),
        grid=(m // tm,),
        in_specs=[rows(a_act.shape[1]), rows(o_gla.shape[1]), rows(o_moba.shape[1]), rows(gates.shape[1]),
                  rows(d), whole(wc), whole(wg), whole(wm), whole(wo)],
        out_specs=rows(d),
        scratch_shapes=[pltpu.VMEM((tm, d), BF16)],
        compiler_params=_cparams(("arbitrary",)),
        name="merge",
    )(a_act, o_gla, o_moba, gates, x, wc, wg, wm, wo)


SUBLANE = 8


def _shift_rows(u, k, carry):
    rolled = pltpu.roll(u, k, 0)
    head = jnp.where(lax.broadcasted_iota(jnp.int32, carry.shape, 0) < k,
                     pltpu.roll(carry, k, 0), rolled[:SUBLANE, :])
    return jnp.concatenate([head, rolled[SUBLANE:, :]], axis=0)


def _ffn_kernel(x_ref, g_ref, wug_ref, wuv_ref, cwg_ref, cwv_ref, wd_ref, o_ref, h_scr, cg, cv,
                *, tiles_per_seq):
    i = pl.program_id(0)
    j = pl.program_id(1)
    tm = x_ref.shape[0]

    @pl.when(j == 0)
    def _():
        x = x_ref[...]
        h_scr[...] = _rms_rows(x, g_ref[...]).astype(BF16)
        o_ref[...] = x

    @pl.when(i % tiles_per_seq == 0)
    def _():
        cg[j] = jnp.zeros(cg.shape[1:], F32)
        cv[j] = jnp.zeros(cv.shape[1:], F32)

    h = h_scr[...]

    def conv(w_ref, cw_ref, carry_ref):
        u = jnp.dot(h, w_ref[...], preferred_element_type=F32)
        carry = carry_ref[j]
        out = (u * cw_ref[2:3, :] + _shift_rows(u, 1, carry) * cw_ref[1:2, :]
               + _shift_rows(u, 2, carry) * cw_ref[0:1, :])
        carry_ref[j] = u[tm - SUBLANE:, :]
        return out

    ug = conv(wug_ref, cwg_ref, cg)
    uv = conv(wuv_ref, cwv_ref, cv)
    act = (_silu(ug) * uv).astype(BF16)
    o_ref[...] += jnp.dot(act, wd_ref[...], preferred_element_type=F32)


def _ffn(x, g, w_up, conv_w, w_down, *, seq, tm=512, tf=512):
    m, d = x.shape
    f = w_down.shape[0]
    nf = f // tf
    return pl.pallas_call(
        functools.partial(_ffn_kernel, tiles_per_seq=seq // tm),
        out_shape=jax.ShapeDtypeStruct((m, d), F32),
        grid=(m // tm, nf),
        in_specs=[
            pl.BlockSpec((tm, d), lambda i, j: (i, 0)),
            pl.BlockSpec((1, d), lambda i, j: (0, 0)),
            pl.BlockSpec((d, tf), lambda i, j: (0, j)),
            pl.BlockSpec((d, tf), lambda i, j: (0, j + nf)),
            pl.BlockSpec((FFN_CONV_WIDTH, tf), lambda i, j: (0, j)),
            pl.BlockSpec((FFN_CONV_WIDTH, tf), lambda i, j: (0, j + nf)),
            pl.BlockSpec((tf, d), lambda i, j: (j, 0)),
        ],
        out_specs=pl.BlockSpec((tm, d), lambda i, j: (i, 0)),
        scratch_shapes=[
            pltpu.VMEM((tm, d), BF16),
            pltpu.VMEM((nf, SUBLANE, tf), F32),
            pltpu.VMEM((nf, SUBLANE, tf), F32),
        ],
        compiler_params=_cparams(("arbitrary", "arbitrary")),
        name="conv_ffn",
    )(x, g, w_up, w_up, conv_w, conv_w, w_down)


def _ple_kernel(x_ref, xc_ref, g_ref, wg_ref, p_ref, wp_ref, o_ref, h_scr):
    @pl.when(pl.program_id(1) == 0)
    def _():
        h_scr[...] = _rms_rows(x_ref[...], g_ref[...]).astype(BF16)

    gate = _sigmoid(jnp.dot(h_scr[...], wg_ref[...], preferred_element_type=F32))
    emb = jnp.dot(p_ref[...].astype(BF16), wp_ref[...], preferred_element_type=F32)
    o_ref[...] = xc_ref[...] + gate * emb


def _ple(x, g, w_gate, p, w_ple, *, tm=1024, tn=1024):
    m, d = x.shape
    return pl.pallas_call(
        _ple_kernel,
        out_shape=jax.ShapeDtypeStruct((m, d), F32),
        grid=(m // tm, d // tn),
        in_specs=[
            pl.BlockSpec((tm, d), lambda i, j: (i, 0)),
            pl.BlockSpec((tm, tn), lambda i, j: (i, j)),
            pl.BlockSpec((1, d), lambda i, j: (0, 0)),
            pl.BlockSpec((d, tn), lambda i, j: (0, j)),
            pl.BlockSpec((tm, PLE_DIM), lambda i, j: (i, 0)),
            pl.BlockSpec((PLE_DIM, tn), lambda i, j: (0, j)),
        ],
        out_specs=pl.BlockSpec((tm, tn), lambda i, j: (i, j)),
        scratch_shapes=[pltpu.VMEM((tm, d), BF16)],
        compiler_params=_cparams(("arbitrary", "arbitrary")),
        name="ple",
    )(x, x, g, w_gate, p, w_ple)


def _pad_in_proj(w_in):
    ga_end = OFF_GA + GLA_GATE_RANK
    pad = jnp.zeros((w_in.shape[0], GATE_PAD - GLA_GATE_RANK), w_in.dtype)
    return jnp.concatenate([w_in[:, :ga_end], pad, w_in[:, ga_end:]], axis=1)


def kernel(x, p, norm_mix_g, w_in, conv_w, conv_b, conv_ln_g, conv_ln_b, w_conv_out, gla_w_a2, gla_b_a,
           gla_norm_g, w_gla_out, moba_q_norm_g, moba_k_norm_g, w_moba_out, w_gate, b_gate, w_out,
           norm_ffn_g, w_up, ffn_conv_w, w_down, norm_ple_g, w_ple_gate, w_ple):
    b_sz, s, d = x.shape
    depth = w_in.shape[0]
    m = b_sz * s
    xf = x.reshape(m, d)
    row = lambda v: v.reshape(1, -1)
    zero_bias = jnp.zeros((1, Z_COLS), F32)
    for i in range(depth):
        w_in_p = _pad_in_proj(w_in[i]).astype(BF16)
        z = _norm_matmul(xf, row(norm_mix_g[i]), w_in_p, zero_bias, sigmoid=False, tm=1024, tn=1152,
                         name="in_proj")
        gates = _norm_matmul(xf, row(norm_mix_g[i]), w_gate[i].astype(BF16), row(b_gate[i]), sigmoid=True,
                             tm=1024, tn=1024, name="gate_proj")
        z3 = z.reshape(b_sz, s, Z_COLS)
        a_act = _conformer_conv(z3, conv_w[i], row(conv_b[i]), row(conv_ln_g[i]), row(conv_ln_b[i]))
        w_a2p = jnp.concatenate(
            [gla_w_a2[i], jnp.zeros((GATE_PAD - GLA_GATE_RANK, GLA_QK), F32)], axis=0).astype(BF16)
        o_gla = _gla(z3, w_a2p, row(gla_b_a[i]), row(gla_norm_g[i]))
        o_moba = _moba(z3, row(moba_q_norm_g[i]), row(moba_k_norm_g[i]))
        xf = _merge(a_act.reshape(m, -1), o_gla.reshape(m, -1), o_moba.reshape(m, -1), gates, xf,
                    w_conv_out[i].astype(BF16), w_gla_out[i].astype(BF16), w_moba_out[i].astype(BF16),
                    w_out[i].astype(BF16))
        xf = _ffn(xf, row(norm_ffn_g[i]), w_up[i].astype(BF16), ffn_conv_w[i], w_down[i].astype(BF16), seq=s)
        xf = _ple(xf, row(norm_ple_g[i]), w_ple_gate[i].astype(BF16), p[i].reshape(m, PLE_DIM),
                  w_ple[i].astype(BF16))
    return xf.reshape(b_sz, s, d)
```

```python
import functools

import jax
import jax.numpy as jnp
from jax import lax
from jax.experimental import pallas as pl
from jax.experimental.pallas import tpu as pltpu

F32 = jnp.float32
BF16 = jnp.bfloat16

D_MODEL = 2048
PLE_DIM = 256
EPS = 1e-6
CONV_CH = 512
CONV_WIDTH = 31
GLA_HEADS = 4
GLA_DV = 128
GLA_DK = 64
GLA_GATE_RANK = 16
GLA_TAU = 16.0
GLA_CHUNK = 64
GLA_SUB = 16
MOBA_HEADS = 8
MOBA_HD = 128
MOBA_BLOCK = 256
MOBA_TOPK = 3
FFN_DIM = 5632
FFN_CONV_WIDTH = 3

LANE = 128
SUBLANE = 8
GATE_PAD = LANE
OFF_GLU = 0
OFF_GQ = OFF_GLU + 2 * CONV_CH
OFF_GK = OFF_GQ + GLA_HEADS * GLA_DK
OFF_GV = OFF_GK + GLA_HEADS * GLA_DK
OFF_GR = OFF_GV + GLA_HEADS * GLA_DV
OFF_GA = OFF_GR + GLA_HEADS * GLA_DV
OFF_MQ = OFF_GA + GATE_PAD
OFF_MK = OFF_MQ + MOBA_HEADS * MOBA_HD
OFF_MV = OFF_MK + MOBA_HEADS * MOBA_HD
Z_COLS = OFF_MV + MOBA_HEADS * MOBA_HD

VMEM_LIMIT = 56 * 1024 * 1024


def _cparams(sem):
    return pltpu.CompilerParams(dimension_semantics=sem, vmem_limit_bytes=VMEM_LIMIT)


def _sigmoid(x):
    return 1.0 / (1.0 + jnp.exp(-x))


def _silu(x):
    return x * _sigmoid(x)


def _rms_rows(x, g):
    ms = jnp.mean(x * x, axis=-1, keepdims=True)
    return x * lax.rsqrt(ms + EPS) * g


def _norm_matmul_kernel(x_ref, g_ref, w_ref, b_ref, o_ref, h_scr, *, sigmoid):
    @pl.when(pl.program_id(1) == 0)
    def _():
        h_scr[...] = _rms_rows(x_ref[...], g_ref[...]).astype(BF16)

    acc = jnp.dot(h_scr[...], w_ref[...], preferred_element_type=F32)
    if sigmoid:
        acc = _sigmoid(acc + b_ref[...])
    o_ref[...] = acc.astype(o_ref.dtype)


def _norm_matmul(x, g, w, b, *, sigmoid, tm, tn, name):
    m, d = x.shape
    n = w.shape[1]
    return pl.pallas_call(
        functools.partial(_norm_matmul_kernel, sigmoid=sigmoid),
        out_shape=jax.ShapeDtypeStruct((m, n), BF16),
        grid=(m // tm, n // tn),
        in_specs=[
            pl.BlockSpec((tm, d), lambda i, j: (i, 0)),
            pl.BlockSpec((1, d), lambda i, j: (0, 0)),
            pl.BlockSpec((d, tn), lambda i, j: (0, j)),
            pl.BlockSpec((1, tn), lambda i, j: (0, j)),
        ],
        out_specs=pl.BlockSpec((tm, tn), lambda i, j: (i, j)),
        scratch_shapes=[pltpu.VMEM((tm, d), BF16)],
        compiler_params=_cparams(("arbitrary", "arbitrary")),
        name=name,
    )(x, g, w, b)


CONV_HALO = 32
CONV_ROWS = 64


def _conv_kernel(z_ref, w_ref, cb_ref, lg_ref, lb_ref, o_ref, abuf, *, ts):
    t = pl.program_id(1)

    @pl.when(t == 0)
    def _():
        abuf[0:CONV_HALO, :] = jnp.zeros((CONV_HALO, CONV_CH), F32)

    @pl.when(t > 0)
    def _():
        abuf[0:CONV_HALO, :] = abuf[ts:ts + CONV_HALO, :]

    z = z_ref[...].astype(F32)
    abuf[CONV_HALO:CONV_HALO + ts, :] = z[:, :CONV_CH] * _sigmoid(z[:, CONV_CH:])

    first = CONV_HALO - (CONV_WIDTH - 1)
    for r0 in range(0, ts, CONV_ROWS):
        acc = abuf[r0 + CONV_HALO:r0 + CONV_HALO + CONV_ROWS, :] * w_ref[CONV_WIDTH - 1:CONV_WIDTH, :]
        for j in range(CONV_WIDTH - 1):
            acc = acc + abuf[r0 + first + j:r0 + first + j + CONV_ROWS, :] * w_ref[j:j + 1, :]
        acc = acc + cb_ref[...]
        mu = jnp.mean(acc, axis=-1, keepdims=True)
        xc = acc - mu
        var = jnp.mean(xc * xc, axis=-1, keepdims=True)
        y = xc * lax.rsqrt(var + EPS) * lg_ref[...] + lb_ref[...]
        o_ref[r0:r0 + CONV_ROWS, :] = _silu(y).astype(o_ref.dtype)


def _conformer_conv(z3, conv_w, conv_b, ln_g, ln_b, *, ts=512):
    b_sz, s, _ = z3.shape
    vec = pl.BlockSpec((1, CONV_CH), lambda b, t: (0, 0))
    return pl.pallas_call(
        functools.partial(_conv_kernel, ts=ts),
        out_shape=jax.ShapeDtypeStruct((b_sz, s, CONV_CH), BF16),
        grid=(b_sz, s // ts),
        in_specs=[
            pl.BlockSpec((None, ts, 2 * CONV_CH), lambda b, t: (b, t, OFF_GLU // (2 * CONV_CH))),
            pl.BlockSpec((CONV_WIDTH, CONV_CH), lambda b, t: (0, 0)),
            vec, vec, vec,
        ],
        out_specs=pl.BlockSpec((None, ts, CONV_CH), lambda b, t: (b, t, 0)),
        scratch_shapes=[pltpu.VMEM((CONV_HALO + ts, CONV_CH), F32)],
        compiler_params=_cparams(("arbitrary", "arbitrary")),
        name="conformer_conv",
    )(z3, conv_w, conv_b, ln_g, ln_b)


GLA_QK = GLA_HEADS * GLA_DK
GLA_V = GLA_HEADS * GLA_DV


def _gla_kernel(q_ref, k_ref, v_ref, r_ref, al_ref, wa_ref, ba_ref, ng_ref, o_ref,
                st, qf, kf, vf, bcs, pbig, intra):
    c = GLA_CHUNK

    @pl.when(pl.program_id(1) == 0)
    def _():
        st[...] = jnp.zeros(st.shape, F32)

    xa = jnp.dot(al_ref[...], wa_ref[...], preferred_element_type=F32) + ba_ref[...]
    la = (jnp.minimum(xa, 0.0) - jnp.log(1.0 + jnp.exp(-jnp.abs(xa)))) * (1.0 / GLA_TAU)
    row = lax.broadcasted_iota(jnp.int32, (c, c), 0)
    col = lax.broadcasted_iota(jnp.int32, (c, c), 1)
    tri = jnp.where(row >= col, 1.0, 0.0).astype(BF16)
    la_hi = la.astype(BF16)
    la_lo = (la - la_hi.astype(F32)).astype(BF16)
    bc = (jnp.dot(tri, la_hi, preferred_element_type=F32)
          + jnp.dot(tri, la_lo, preferred_element_type=F32))

    q = q_ref[...].astype(F32) * (GLA_DK ** -0.5)
    k = k_ref[...].astype(F32)
    qf[...] = q
    kf[...] = k
    vf[...] = v_ref[...].astype(F32)
    bcs[...] = bc

    b_last = bc[c - 1:c, :]
    q_in = (q * jnp.exp(bc)).astype(BF16)
    k_dec = (k * jnp.exp(b_last - bc)).astype(BF16)
    e_last = jnp.exp(b_last)
    v_bf = v_ref[...]

    srow = lax.broadcasted_iota(jnp.int32, (GLA_QK, GLA_V), 0) // GLA_DK
    scol = lax.broadcasted_iota(jnp.int32, (GLA_QK, GLA_V), 1) // GLA_DV
    sel = jnp.where(srow == scol, 1.0, 0.0).astype(BF16)

    for j in range(c // GLA_SUB):
        t0 = j * GLA_SUB
        tj = c - t0
        q_rows = qf[t0:c, :]
        bc_rows = bcs[t0:c, :]
        for sl in range(GLA_SUB):
            s = t0 + sl
            e = jnp.exp(jnp.minimum(bc_rows - bcs[s:s + 1, :], 0.0))
            pbig[sl * tj:(sl + 1) * tj, :] = ((q_rows * kf[s:s + 1, :]) * e).astype(BF16)
        r_all = jnp.dot(pbig[0:GLA_SUB * tj, :], sel, preferred_element_type=F32)
        t_loc = lax.broadcasted_iota(jnp.int32, (tj, GLA_V), 0)
        acc = jnp.zeros((tj, GLA_V), F32)
        for sl in range(GLA_SUB):
            s = t0 + sl
            blk = r_all[sl * tj:(sl + 1) * tj, :]
            blk = jnp.where(t_loc >= sl, blk, 0.0)
            acc = acc + blk * vf[s:s + 1, :]
        if j == 0:
            intra[...] = acc
        else:
            intra[t0:c, :] = intra[t0:c, :] + acc

    for h in range(GLA_HEADS):
        ks = slice(h * GLA_DK, (h + 1) * GLA_DK)
        vs = slice(h * GLA_DV, (h + 1) * GLA_DV)
        st_h = st[h]
        inter = lax.dot_general(q_in[:, ks], st_h.astype(BF16), (((1,), (1,)), ((), ())),
                                preferred_element_type=F32)
        upd = lax.dot_general(v_bf[:, vs], k_dec[:, ks], (((0,), (0,)), ((), ())),
                              preferred_element_type=F32)
        st[h] = st_h * e_last[:, ks] + upd
        o_h = inter + intra[:, vs]
        y = _rms_rows(o_h, ng_ref[...])
        o_ref[:, vs] = (y * _silu(r_ref[:, vs].astype(F32))).astype(o_ref.dtype)


def _gla(z3, w_a2p, b_a, norm_g):
    b_sz, s, _ = z3.shape
    c = GLA_CHUNK

    def zspec(width, off):
        return pl.BlockSpec((None, c, width), lambda b, t: (b, t, off // width))

    return pl.pallas_call(
        _gla_kernel,
        out_shape=jax.ShapeDtypeStruct((b_sz, s, GLA_V), BF16),
        grid=(b_sz, s // c),
        in_specs=[
            zspec(GLA_QK, OFF_GQ), zspec(GLA_QK, OFF_GK), zspec(GLA_V, OFF_GV), zspec(GLA_V, OFF_GR),
            zspec(GATE_PAD, OFF_GA),
            pl.BlockSpec((GATE_PAD, GLA_QK), lambda b, t: (0, 0)),
            pl.BlockSpec((1, GLA_QK), lambda b, t: (0, 0)),
            pl.BlockSpec((1, GLA_DV), lambda b, t: (0, 0)),
        ],
        out_specs=pl.BlockSpec((None, c, GLA_V), lambda b, t: (b, t, 0)),
        scratch_shapes=[
            pltpu.VMEM((GLA_HEADS, GLA_DV, GLA_DK), F32),
            pltpu.VMEM((c, GLA_QK), F32),
            pltpu.VMEM((c, GLA_QK), F32),
            pltpu.VMEM((c, GLA_V), F32),
            pltpu.VMEM((c, GLA_QK), F32),
            pltpu.VMEM((GLA_SUB * c, GLA_QK), BF16),
            pltpu.VMEM((c, GLA_V), F32),
        ],
        compiler_params=_cparams(("arbitrary", "arbitrary")),
        name="gla",
    )(z3, z3, z3, z3, z3, w_a2p, b_a, norm_g)


NEG_INF = float("-inf")
MOBA_GROUP = 4
NT_DIMS = (((1,), (1,)), ((), ()))


def _fold_rows(x, op):
    return op(x.reshape(x.shape[0] // SUBLANE, SUBLANE, x.shape[1]), axis=0)


def _moba_kernel(q_ref, k_ref, v_ref, gq_ref, gk_ref, o_ref, kn, vt, kmean, chosen, s_scr, s_own,
                 *, n_blk):
    qi = pl.program_id(2)
    blk = MOBA_BLOCK
    grp = MOBA_GROUP

    @pl.when(qi == 0)
    def _():
        for n in range(n_blk):
            kb = _rms_rows(k_ref[n * blk:(n + 1) * blk, :].astype(F32), gk_ref[...])
            kn[n * blk:(n + 1) * blk, :] = kb.astype(BF16)
            kmean[n:n + 1, :] = jnp.mean(kb, axis=0, keepdims=True)
            vt[n] = v_ref[n * blk:(n + 1) * blk, :].astype(F32).T.astype(BF16)

    qn = _rms_rows(q_ref[...].astype(F32), gq_ref[...])
    q_bf = (qn * (MOBA_HD ** -0.5)).astype(BF16)

    sb = lax.dot_general(kmean[...], qn, NT_DIMS, precision=lax.Precision.HIGHEST,
                         preferred_element_type=F32)
    brow = lax.broadcasted_iota(jnp.int32, (n_blk, blk), 0)
    past = brow < qi
    sb = jnp.where(past, sb, NEG_INF)
    rank = jnp.zeros((n_blk, blk), F32)
    for m in range(n_blk):
        cm = sb[m:m + 1, :]
        ge = jnp.where(cm >= sb, 1.0, 0.0)
        gt = jnp.where(cm > sb, 1.0, 0.0)
        rank = rank + jnp.where(brow > m, ge, gt)
    chosen[...] = jnp.where(past, jnp.where(rank < float(MOBA_TOPK), 1.0, 0.0), 0.0)

    own = pl.multiple_of(qi * blk, blk)
    so = lax.dot_general(kn[pl.ds(own, blk), :], q_bf, NT_DIMS, preferred_element_type=F32)
    k_i = lax.broadcasted_iota(jnp.int32, (blk, blk), 0)
    q_i = lax.broadcasted_iota(jnp.int32, (blk, blk), 1)
    so = jnp.where(k_i <= q_i, so, NEG_INF)
    s_own[...] = so
    n_grp = (qi + grp - 1) // grp

    def scores(g, m_run):
        start = pl.multiple_of(g * (grp * blk), grp * blk)
        sg = lax.dot_general(kn[pl.ds(start, grp * blk), :], q_bf, NT_DIMS,
                             preferred_element_type=F32)
        for b in range(grp):
            n = g * grp + b
            sn = jnp.where(chosen[pl.ds(n, 1), :] > 0.0, sg[b * blk:(b + 1) * blk, :], NEG_INF)
            s_scr[n] = sn
            m_run = jnp.maximum(m_run, _fold_rows(sn, jnp.max))
        return m_run

    m_run = lax.fori_loop(0, n_grp, scores, _fold_rows(so, jnp.max))
    m_row = jnp.max(m_run, axis=0, keepdims=True)

    p_own = jnp.exp(s_own[...] - m_row)
    l_run0 = _fold_rows(p_own, jnp.sum)
    acc0 = jnp.dot(vt[qi], p_own.astype(BF16), preferred_element_type=F32)

    def values(g, carry):
        l_run, acc = carry
        ps, vs = [], []
        for b in range(grp):
            n = g * grp + b
            pn = jnp.exp(s_scr[n] - m_row)
            l_run = l_run + _fold_rows(pn, jnp.sum)
            ps.append(pn.astype(BF16))
            vs.append(vt[n])
        acc = acc + jnp.dot(jnp.concatenate(vs, axis=1), jnp.concatenate(ps, axis=0),
                            preferred_element_type=F32)
        return l_run, acc

    l_run, acc = lax.fori_loop(0, n_grp, values, (l_run0, acc0))
    l_row = jnp.sum(l_run, axis=0, keepdims=True)
    o_ref[...] = (acc / l_row).T.astype(o_ref.dtype)


def _moba(z3, gq, gk):
    b_sz, s, _ = z3.shape
    n_blk = s // MOBA_BLOCK
    hd = MOBA_HD
    return pl.pallas_call(
        functools.partial(_moba_kernel, n_blk=n_blk),
        out_shape=jax.ShapeDtypeStruct((b_sz, s, MOBA_HEADS * hd), BF16),
        grid=(b_sz, MOBA_HEADS, n_blk),
        in_specs=[
            pl.BlockSpec((None, MOBA_BLOCK, hd), lambda b, h, i: (b, i, OFF_MQ // hd + h)),
            pl.BlockSpec((None, s, hd), lambda b, h, i: (b, 0, OFF_MK // hd + h)),
            pl.BlockSpec((None, s, hd), lambda b, h, i: (b, 0, OFF_MV // hd + h)),
            pl.BlockSpec((1, hd), lambda b, h, i: (0, 0)),
            pl.BlockSpec((1, hd), lambda b, h, i: (0, 0)),
        ],
        out_specs=pl.BlockSpec((None, MOBA_BLOCK, hd), lambda b, h, i: (b, i, h)),
        scratch_shapes=[
            pltpu.VMEM((s, hd), BF16),
            pltpu.VMEM((n_blk, hd, MOBA_BLOCK), BF16),
            pltpu.VMEM((n_blk, hd), F32),
            pltpu.VMEM((n_blk, MOBA_BLOCK), F32),
            pltpu.VMEM((n_blk, MOBA_BLOCK, MOBA_BLOCK), F32),
            pltpu.VMEM((MOBA_BLOCK, MOBA_BLOCK), F32),
        ],
        compiler_params=_cparams(("arbitrary", "arbitrary", "arbitrary")),
        name="moba",
    )(z3, z3, z3, gq, gk)


MERGE_COLS = 512


def _merge_kernel(a_ref, gl_ref, mo_ref, gt_ref, x_ref, wc_ref, wg_ref, wm_ref, wo_ref, o_ref, mg):
    d = D_MODEL
    for c0 in range(0, d, MERGE_COLS):
        cs = slice(c0, c0 + MERGE_COLS)
        ya = jnp.dot(a_ref[...], wc_ref[:, cs], preferred_element_type=F32)
        yb = jnp.dot(gl_ref[...], wg_ref[:, cs], preferred_element_type=F32)
        yc = jnp.dot(mo_ref[...], wm_ref[:, cs], preferred_element_type=F32)
        g0 = gt_ref[:, c0:c0 + MERGE_COLS].astype(F32)
        g1 = gt_ref[:, d + c0:d + c0 + MERGE_COLS].astype(F32)
        g2 = gt_ref[:, 2 * d + c0:2 * d + c0 + MERGE_COLS].astype(F32)
        mg[:, cs] = (g0 * ya + g1 * yb + g2 * yc).astype(BF16)
    o_ref[...] = x_ref[...] + jnp.dot(mg[...], wo_ref[...], preferred_element_type=F32)


def _merge(a_act, o_gla, o_moba, gates, x, wc, wg, wm, wo, *, tm=256):
    m, d = x.shape

    def rows(width):
        return pl.BlockSpec((tm, width), lambda i: (i, 0))

    def whole(w):
        return pl.BlockSpec(w.shape, lambda i: (0, 0), pipeline_mode=pl.Buffered(1))

    return pl.pallas_call(
        _merge_kernel,
        out_shape=jax.ShapeDtypeStruct((m, d), F32),
        grid=(m // tm,),
        in_specs=[rows(a_act.shape[1]), rows(o_gla.shape[1]), rows(o_moba.shape[1]), rows(gates.shape[1]),
                  rows(d), whole(wc), whole(wg), whole(wm), whole(wo)],
        out_specs=rows(d),
        scratch_shapes=[pltpu.VMEM((tm, d), BF16)],
        compiler_params=_cparams(("arbitrary",)),
        name="merge",
    )(a_act, o_gla, o_moba, gates, x, wc, wg, wm, wo)


def _shift_rows(u, k, carry):
    rolled = pltpu.roll(u, k, 0)
    head = jnp.where(lax.broadcasted_iota(jnp.int32, carry.shape, 0) < k,
                     pltpu.roll(carry, k, 0), rolled[:SUBLANE, :])
    return jnp.concatenate([head, rolled[SUBLANE:, :]], axis=0)


def _ffn_kernel(x_ref, g_ref, wug_ref, wuv_ref, cwg_ref, cwv_ref, wd_ref, o_ref, h_scr, cg, cv,
                *, tiles_per_seq):
    i = pl.program_id(0)
    j = pl.program_id(1)
    tm = x_ref.shape[0]

    @pl.when(j == 0)
    def _():
        x = x_ref[...]
        h_scr[...] = _rms_rows(x, g_ref[...]).astype(BF16)
        o_ref[...] = x

    @pl.when(i % tiles_per_seq == 0)
    def _():
        cg[j] = jnp.zeros(cg.shape[1:], F32)
        cv[j] = jnp.zeros(cv.shape[1:], F32)

    h = h_scr[...]

    def conv(w_ref, cw_ref, carry_ref):
        u = jnp.dot(h, w_ref[...], preferred_element_type=F32)
        carry = carry_ref[j]
        out = (u * cw_ref[2:3, :] + _shift_rows(u, 1, carry) * cw_ref[1:2, :]
               + _shift_rows(u, 2, carry) * cw_ref[0:1, :])
        carry_ref[j] = u[tm - SUBLANE:, :]
        return out

    ug = conv(wug_ref, cwg_ref, cg)
    uv = conv(wuv_ref, cwv_ref, cv)
    act = (_silu(ug) * uv).astype(BF16)
    o_ref[...] += jnp.dot(act, wd_ref[...], preferred_element_type=F32)


def _ffn(x, g, w_up, conv_w, w_down, *, seq, tm=512, tf=512):
    m, d = x.shape
    f = w_down.shape[0]
    nf = f // tf
    return pl.pallas_call(
        functools.partial(_ffn_kernel, tiles_per_seq=seq // tm),
        out_shape=jax.ShapeDtypeStruct((m, d), F32),
        grid=(m // tm, nf),
        in_specs=[
            pl.BlockSpec((tm, d), lambda i, j: (i, 0)),
            pl.BlockSpec((1, d), lambda i, j: (0, 0)),
            pl.BlockSpec((d, tf), lambda i, j: (0, j)),
            pl.BlockSpec((d, tf), lambda i, j: (0, j + nf)),
            pl.BlockSpec((FFN_CONV_WIDTH, tf), lambda i, j: (0, j)),
            pl.BlockSpec((FFN_CONV_WIDTH, tf), lambda i, j: (0, j + nf)),
            pl.BlockSpec((tf, d), lambda i, j: (j, 0)),
        ],
        out_specs=pl.BlockSpec((tm, d), lambda i, j: (i, 0)),
        scratch_shapes=[
            pltpu.VMEM((tm, d), BF16),
            pltpu.VMEM((nf, SUBLANE, tf), F32),
            pltpu.VMEM((nf, SUBLANE, tf), F32),
        ],
        compiler_params=_cparams(("arbitrary", "arbitrary")),
        name="conv_ffn",
    )(x, g, w_up, w_up, conv_w, conv_w, w_down)


def _ple_kernel(x_ref, xc_ref, g_ref, wg_ref, p_ref, wp_ref, o_ref, h_scr):
    @pl.when(pl.program_id(1) == 0)
    def _():
        h_scr[...] = _rms_rows(x_ref[...], g_ref[...]).astype(BF16)

    gate = _sigmoid(jnp.dot(h_scr[...], wg_ref[...], preferred_element_type=F32))
    emb = jnp.dot(p_ref[...].astype(BF16), wp_ref[...], preferred_element_type=F32)
    o_ref[...] = xc_ref[...] + gate * emb


def _ple(x, g, w_gate, p, w_ple, *, tm=1024, tn=1024):
    m, d = x.shape
    return pl.pallas_call(
        _ple_kernel,
        out_shape=jax.ShapeDtypeStruct((m, d), F32),
        grid=(m // tm, d // tn),
        in_specs=[
            pl.BlockSpec((tm, d), lambda i, j: (i, 0)),
            pl.BlockSpec((tm, tn), lambda i, j: (i, j)),
            pl.BlockSpec((1, d), lambda i, j: (0, 0)),
            pl.BlockSpec((d, tn), lambda i, j: (0, j)),
            pl.BlockSpec((tm, PLE_DIM), lambda i, j: (i, 0)),
            pl.BlockSpec((PLE_DIM, tn), lambda i, j: (0, j)),
        ],
        out_specs=pl.BlockSpec((tm, tn), lambda i, j: (i, j)),
        scratch_shapes=[pltpu.VMEM((tm, d), BF16)],
        compiler_params=_cparams(("arbitrary", "arbitrary")),
        name="ple",
    )(x, x, g, w_gate, p, w_ple)


def _pad_in_proj(w_in):
    ga_end = OFF_GA + GLA_GATE_RANK
    pad = jnp.zeros((w_in.shape[0], GATE_PAD - GLA_GATE_RANK), w_in.dtype)
    return jnp.concatenate([w_in[:, :ga_end], pad, w_in[:, ga_end:]], axis=1)


def kernel(x, p, norm_mix_g, w_in, conv_w, conv_b, conv_ln_g, conv_ln_b, w_conv_out, gla_w_a2, gla_b_a,
           gla_norm_g, w_gla_out, moba_q_norm_g, moba_k_norm_g, w_moba_out, w_gate, b_gate, w_out,
           norm_ffn_g, w_up, ffn_conv_w, w_down, norm_ple_g, w_ple_gate, w_ple):
    b_sz, s, d = x.shape
    depth = w_in.shape[0]
    m = b_sz * s
    xf = x.reshape(m, d)
    row = lambda v: v.reshape(1, -1)
    zero_bias = jnp.zeros((1, Z_COLS), F32)
    for i in range(depth):
        w_in_p = _pad_in_proj(w_in[i]).astype(BF16)
        z = _norm_matmul(xf, row(norm_mix_g[i]), w_in_p, zero_bias, sigmoid=False, tm=1024, tn=1152,
                         name="in_proj")
        gates = _norm_matmul(xf, row(norm_mix_g[i]), w_gate[i].astype(BF16), row(b_gate[i]), sigmoid=True,
                             tm=1024, tn=1024, name="gate_proj")
        z3 = z.reshape(b_sz, s, Z_COLS)
        a_act = _conformer_conv(z3, conv_w[i], row(conv_b[i]), row(conv_ln_g[i]), row(conv_ln_b[i]))
        w_a2p = jnp.concatenate(
            [gla_w_a2[i], jnp.zeros((GATE_PAD - GLA_GATE_RANK, GLA_QK), F32)], axis=0).astype(BF16)
        o_gla = _gla(z3, w_a2p, row(gla_b_a[i]), row(gla_norm_g[i]))
        o_moba = _moba(z3, row(moba_q_norm_g[i]), row(moba_k_norm_g[i]))
        xf = _merge(a_act.reshape(m, -1), o_gla.reshape(m, -1), o_moba.reshape(m, -1), gates, xf,
                    w_conv_out[i].astype(BF16), w_gla_out[i].astype(BF16), w_moba_out[i].astype(BF16),
                    w_out[i].astype(BF16))
        xf = _ffn(xf, row(norm_ffn_g[i]), w_up[i].astype(BF16), ffn_conv_w[i], w_down[i].astype(BF16), seq=s)
        xf = _ple(xf, row(norm_ple_g[i]), w_ple_gate[i].astype(BF16), p[i].reshape(m, PLE_DIM),
                  w_ple[i].astype(BF16))
    return xf.reshape(b_sz, s, d)
```

```python
import functools

import jax
import jax.numpy as jnp
from jax import lax
from jax.experimental import pallas as pl
from jax.experimental.pallas import tpu as pltpu

F32 = jnp.float32
BF16 = jnp.bfloat16

D_MODEL = 2048
PLE_DIM = 256
EPS = 1e-6
CONV_CH = 512
CONV_WIDTH = 31
GLA_HEADS = 4
GLA_DV = 128
GLA_DK = 64
GLA_GATE_RANK = 16
GLA_TAU = 16.0
GLA_CHUNK = 64
GLA_SUB = 16
MOBA_HEADS = 8
MOBA_HD = 128
MOBA_BLOCK = 256
MOBA_TOPK = 3
FFN_DIM = 5632
FFN_CONV_WIDTH = 3

LANE = 128
SUBLANE = 8
GATE_PAD = LANE
OFF_GLU = 0
OFF_GQ = OFF_GLU + 2 * CONV_CH
OFF_GK = OFF_GQ + GLA_HEADS * GLA_DK
OFF_GV = OFF_GK + GLA_HEADS * GLA_DK
OFF_GR = OFF_GV + GLA_HEADS * GLA_DV
OFF_GA = OFF_GR + GLA_HEADS * GLA_DV
OFF_MQ = OFF_GA + GATE_PAD
OFF_MK = OFF_MQ + MOBA_HEADS * MOBA_HD
OFF_MV = OFF_MK + MOBA_HEADS * MOBA_HD
Z_COLS = OFF_MV + MOBA_HEADS * MOBA_HD

VMEM_LIMIT = 56 * 1024 * 1024


def _cparams(sem):
    return pltpu.CompilerParams(dimension_semantics=sem, vmem_limit_bytes=VMEM_LIMIT)


def _sigmoid(x):
    return 1.0 / (1.0 + jnp.exp(-x))


def _silu(x):
    return x * _sigmoid(x)


def _rms_rows(x, g):
    ms = jnp.mean(x * x, axis=-1, keepdims=True)
    return x * lax.rsqrt(ms + EPS) * g


def _layer_vec(width, layer):
    return pl.BlockSpec((None, 1, width), lambda i: (layer, 0, 0))


def _rmsnorm_kernel(x_ref, g_ref, o_ref):
    o_ref[...] = _rms_rows(x_ref[...], g_ref[...]).astype(o_ref.dtype)


def _rmsnorm(x, g_stack, layer, *, tm=1024):
    m, d = x.shape
    return pl.pallas_call(
        _rmsnorm_kernel,
        out_shape=jax.ShapeDtypeStruct((m, d), BF16),
        grid=(m // tm,),
        in_specs=[pl.BlockSpec((tm, d), lambda i: (i, 0)), _layer_vec(d, layer)],
        out_specs=pl.BlockSpec((tm, d), lambda i: (i, 0)),
        compiler_params=_cparams(("arbitrary",)),
        name="rmsnorm",
    )(x, g_stack)


def _proj_kernel(h_ref, w_ref, o_ref):
    o_ref[...] = jnp.dot(h_ref[...], w_ref[...], preferred_element_type=F32).astype(o_ref.dtype)


def _proj_sigmoid_kernel(h_ref, w_ref, b_ref, o_ref):
    acc = jnp.dot(h_ref[...], w_ref[...], preferred_element_type=F32)
    o_ref[...] = _sigmoid(acc + b_ref[...]).astype(o_ref.dtype)


def _proj(h, w_stack, b_stack, layer, *, tm, tn, name):
    m, d = h.shape
    n = w_stack.shape[2]
    in_specs = [pl.BlockSpec((tm, d), lambda i, j: (i, 0)),
                pl.BlockSpec((None, d, tn), lambda i, j: (layer, 0, j))]
    args = [h, w_stack]
    if b_stack is not None:
        in_specs.append(pl.BlockSpec((None, 1, tn), lambda i, j: (layer, 0, j)))
        args.append(b_stack)
    return pl.pallas_call(
        _proj_kernel if b_stack is None else _proj_sigmoid_kernel,
        out_shape=jax.ShapeDtypeStruct((m, n), BF16),
        grid=(m // tm, n // tn),
        in_specs=in_specs,
        out_specs=pl.BlockSpec((tm, tn), lambda i, j: (i, j)),
        compiler_params=_cparams(("arbitrary", "arbitrary")),
        name=name,
    )(*args)


CONV_HALO = 32
CONV_ROWS = 64


def _conv_kernel(z_ref, w_ref, cb_ref, lg_ref, lb_ref, o_ref, abuf, *, ts):
    t = pl.program_id(1)

    @pl.when(t == 0)
    def _():
        abuf[0:CONV_HALO, :] = jnp.zeros((CONV_HALO, CONV_CH), F32)

    @pl.when(t > 0)
    def _():
        abuf[0:CONV_HALO, :] = abuf[ts:ts + CONV_HALO, :]

    z = z_ref[...].astype(F32)
    abuf[CONV_HALO:CONV_HALO + ts, :] = z[:, :CONV_CH] * _sigmoid(z[:, CONV_CH:])

    first = CONV_HALO - (CONV_WIDTH - 1)
    for r0 in range(0, ts, CONV_ROWS):
        acc = abuf[r0 + CONV_HALO:r0 + CONV_HALO + CONV_ROWS, :] * w_ref[CONV_WIDTH - 1:CONV_WIDTH, :]
        for j in range(CONV_WIDTH - 1):
            acc = acc + abuf[r0 + first + j:r0 + first + j + CONV_ROWS, :] * w_ref[j:j + 1, :]
        acc = acc + cb_ref[...]
        mu = jnp.mean(acc, axis=-1, keepdims=True)
        xc = acc - mu
        var = jnp.mean(xc * xc, axis=-1, keepdims=True)
        y = xc * lax.rsqrt(var + EPS) * lg_ref[...] + lb_ref[...]
        o_ref[r0:r0 + CONV_ROWS, :] = _silu(y).astype(o_ref.dtype)


def _conformer_conv(z3, conv_w, conv_b, ln_g, ln_b, *, ts=512):
    b_sz, s, _ = z3.shape
    vec = pl.BlockSpec((1, CONV_CH), lambda b, t: (0, 0))
    return pl.pallas_call(
        functools.partial(_conv_kernel, ts=ts),
        out_shape=jax.ShapeDtypeStruct((b_sz, s, CONV_CH), BF16),
        grid=(b_sz, s // ts),
        in_specs=[
            pl.BlockSpec((None, ts, 2 * CONV_CH), lambda b, t: (b, t, OFF_GLU // (2 * CONV_CH))),
            pl.BlockSpec((CONV_WIDTH, CONV_CH), lambda b, t: (0, 0)),
            vec, vec, vec,
        ],
        out_specs=pl.BlockSpec((None, ts, CONV_CH), lambda b, t: (b, t, 0)),
        scratch_shapes=[pltpu.VMEM((CONV_HALO + ts, CONV_CH), F32)],
        compiler_params=_cparams(("arbitrary", "arbitrary")),
        name="conformer_conv",
    )(z3, conv_w, conv_b, ln_g, ln_b)


GLA_QK = GLA_HEADS * GLA_DK
GLA_V = GLA_HEADS * GLA_DV


def _gla_kernel(q_ref, k_ref, v_ref, r_ref, al_ref, wa_ref, ba_ref, ng_ref, o_ref,
                st, qf, kf, vf, bcs, pbig, intra):
    c = GLA_CHUNK

    @pl.when(pl.program_id(1) == 0)
    def _():
        st[...] = jnp.zeros(st.shape, F32)

    xa = jnp.dot(al_ref[...], wa_ref[...], preferred_element_type=F32) + ba_ref[...]
    la = (jnp.minimum(xa, 0.0) - jnp.log(1.0 + jnp.exp(-jnp.abs(xa)))) * (1.0 / GLA_TAU)
    row = lax.broadcasted_iota(jnp.int32, (c, c), 0)
    col = lax.broadcasted_iota(jnp.int32, (c, c), 1)
    tri = jnp.where(row >= col, 1.0, 0.0).astype(BF16)
    la_hi = la.astype(BF16)
    la_lo = (la - la_hi.astype(F32)).astype(BF16)
    bc = (jnp.dot(tri, la_hi, preferred_element_type=F32)
          + jnp.dot(tri, la_lo, preferred_element_type=F32))

    q = q_ref[...].astype(F32) * (GLA_DK ** -0.5)
    k = k_ref[...].astype(F32)
    qf[...] = q
    kf[...] = k
    vf[...] = v_ref[...].astype(F32)
    bcs[...] = bc

    b_last = bc[c - 1:c, :]
    q_in = (q * jnp.exp(bc)).astype(BF16)
    k_dec = (k * jnp.exp(b_last - bc)).astype(BF16)
    e_last = jnp.exp(b_last)
    v_bf = v_ref[...]

    srow = lax.broadcasted_iota(jnp.int32, (GLA_QK, GLA_V), 0) // GLA_DK
    scol = lax.broadcasted_iota(jnp.int32, (GLA_QK, GLA_V), 1) // GLA_DV
    sel = jnp.where(srow == scol, 1.0, 0.0).astype(BF16)

    for j in range(c // GLA_SUB):
        t0 = j * GLA_SUB
        tj = c - t0
        q_rows = qf[t0:c, :]
        bc_rows = bcs[t0:c, :]
        for sl in range(GLA_SUB):
            s = t0 + sl
            e = jnp.exp(jnp.minimum(bc_rows - bcs[s:s + 1, :], 0.0))
            pbig[sl * tj:(sl + 1) * tj, :] = ((q_rows * kf[s:s + 1, :]) * e).astype(BF16)
        r_all = jnp.dot(pbig[0:GLA_SUB * tj, :], sel, preferred_element_type=F32)
        t_loc = lax.broadcasted_iota(jnp.int32, (tj, GLA_V), 0)
        acc = jnp.zeros((tj, GLA_V), F32)
        for sl in range(GLA_SUB):
            s = t0 + sl
            blk = r_all[sl * tj:(sl + 1) * tj, :]
            blk = jnp.where(t_loc >= sl, blk, 0.0)
            acc = acc + blk * vf[s:s + 1, :]
        if j == 0:
            intra[...] = acc
        else:
            intra[t0:c, :] = intra[t0:c, :] + acc

    for h in range(GLA_HEADS):
        ks = slice(h * GLA_DK, (h + 1) * GLA_DK)
        vs = slice(h * GLA_DV, (h + 1) * GLA_DV)
        st_h = st[h]
        inter = lax.dot_general(q_in[:, ks], st_h.astype(BF16), (((1,), (1,)), ((), ())),
                                preferred_element_type=F32)
        upd = lax.dot_general(v_bf[:, vs], k_dec[:, ks], (((0,), (0,)), ((), ())),
                              preferred_element_type=F32)
        st[h] = st_h * e_last[:, ks] + upd
        o_h = inter + intra[:, vs]
        y = _rms_rows(o_h, ng_ref[...])
        o_ref[:, vs] = (y * _silu(r_ref[:, vs].astype(F32))).astype(o_ref.dtype)


def _gla(z3, w_a2p, b_a, norm_g):
    b_sz, s, _ = z3.shape
    c = GLA_CHUNK

    def zspec(width, off):
        return pl.BlockSpec((None, c, width), lambda b, t: (b, t, off // width))

    return pl.pallas_call(
        _gla_kernel,
        out_shape=jax.ShapeDtypeStruct((b_sz, s, GLA_V), BF16),
        grid=(b_sz, s // c),
        in_specs=[
            zspec(GLA_QK, OFF_GQ), zspec(GLA_QK, OFF_GK), zspec(GLA_V, OFF_GV), zspec(GLA_V, OFF_GR),
            zspec(GATE_PAD, OFF_GA),
            pl.BlockSpec((GATE_PAD, GLA_QK), lambda b, t: (0, 0)),
            pl.BlockSpec((1, GLA_QK), lambda b, t: (0, 0)),
            pl.BlockSpec((1, GLA_DV), lambda b, t: (0, 0)),
        ],
        out_specs=pl.BlockSpec((None, c, GLA_V), lambda b, t: (b, t, 0)),
        scratch_shapes=[
            pltpu.VMEM((GLA_HEADS, GLA_DV, GLA_DK), F32),
            pltpu.VMEM((c, GLA_QK), F32),
            pltpu.VMEM((c, GLA_QK), F32),
            pltpu.VMEM((c, GLA_V), F32),
            pltpu.VMEM((c, GLA_QK), F32),
            pltpu.VMEM((GLA_SUB * c, GLA_QK), BF16),
            pltpu.VMEM((c, GLA_V), F32),
        ],
        compiler_params=_cparams(("arbitrary", "arbitrary")),
        name="gla",
    )(z3, z3, z3, z3, z3, w_a2p, b_a, norm_g)


NEG_INF = float("-inf")
MOBA_GROUP = 4
NT_DIMS = (((1,), (1,)), ((), ()))


def _fold_rows(x, op):
    return op(x.reshape(x.shape[0] // SUBLANE, SUBLANE, x.shape[1]), axis=0)


MOBA_HPS = 2


def _moba_kernel(*refs, n_blk):
    hps = MOBA_HPS
    q_refs, k_refs, v_refs = refs[:hps], refs[hps:2 * hps], refs[2 * hps:3 * hps]
    gq_ref, gk_ref, o_ref, kn, vt, kmean, chosen, s_scr, s_own = refs[3 * hps:]
    qi = pl.program_id(2)
    blk = MOBA_BLOCK
    grp = MOBA_GROUP
    heads = range(hps)

    @pl.when(qi == 0)
    def _():
        for hh in heads:
            for n in range(n_blk):
                kb = _rms_rows(k_refs[hh][n * blk:(n + 1) * blk, :].astype(F32), gk_ref[...])
                kn[hh, n * blk:(n + 1) * blk, :] = kb.astype(BF16)
                kmean[hh, n:n + 1, :] = jnp.mean(kb, axis=0, keepdims=True)
                vt[hh, n] = v_refs[hh][n * blk:(n + 1) * blk, :].astype(F32).T.astype(BF16)

    brow = lax.broadcasted_iota(jnp.int32, (n_blk, blk), 0)
    past = brow < qi
    k_i = lax.broadcasted_iota(jnp.int32, (blk, blk), 0)
    q_i = lax.broadcasted_iota(jnp.int32, (blk, blk), 1)
    own = pl.multiple_of(qi * blk, blk)
    q_bf, m_run0 = [], []
    for hh in heads:
        qn = _rms_rows(q_refs[hh][...].astype(F32), gq_ref[...])
        q_bf.append((qn * (MOBA_HD ** -0.5)).astype(BF16))
        sb = lax.dot_general(kmean[hh], qn, NT_DIMS, precision=lax.Precision.HIGHEST,
                             preferred_element_type=F32)
        sb = jnp.where(past, sb, NEG_INF)
        rank = jnp.zeros((n_blk, blk), F32)
        for m in range(n_blk):
            cm = sb[m:m + 1, :]
            ge = jnp.where(cm >= sb, 1.0, 0.0)
            gt = jnp.where(cm > sb, 1.0, 0.0)
            rank = rank + jnp.where(brow > m, ge, gt)
        chosen[hh] = jnp.where(past, jnp.where(rank < float(MOBA_TOPK), 1.0, 0.0), 0.0)
        so = lax.dot_general(kn[hh, pl.ds(own, blk), :], q_bf[hh], NT_DIMS, preferred_element_type=F32)
        so = jnp.where(k_i <= q_i, so, NEG_INF)
        s_own[hh] = so
        m_run0.append(_fold_rows(so, jnp.max))
    n_grp = (qi + grp - 1) // grp

    def scores(g, m_runs):
        start = pl.multiple_of(g * (grp * blk), grp * blk)
        out = []
        for hh in heads:
            m_run = m_runs[hh]
            sg = lax.dot_general(kn[hh, pl.ds(start, grp * blk), :], q_bf[hh], NT_DIMS,
                                 preferred_element_type=F32)
            for b in range(grp):
                n = g * grp + b
                sn = jnp.where(chosen[hh, pl.ds(n, 1), :] > 0.0, sg[b * blk:(b + 1) * blk, :], NEG_INF)
                s_scr[hh, n] = sn
                m_run = jnp.maximum(m_run, _fold_rows(sn, jnp.max))
            out.append(m_run)
        return tuple(out)

    m_runs = lax.fori_loop(0, n_grp, scores, tuple(m_run0))
    m_rows = [jnp.max(m_runs[hh], axis=0, keepdims=True) for hh in heads]

    carry0 = []
    for hh in heads:
        p_own = jnp.exp(s_own[hh] - m_rows[hh])
        carry0.append(_fold_rows(p_own, jnp.sum))
        carry0.append(jnp.dot(vt[hh, qi], p_own.astype(BF16), preferred_element_type=F32))

    def values(g, carry):
        out = []
        for hh in heads:
            l_run, acc = carry[2 * hh], carry[2 * hh + 1]
            ps, vs = [], []
            for b in range(grp):
                n = g * grp + b
                pn = jnp.exp(s_scr[hh, n] - m_rows[hh])
                l_run = l_run + _fold_rows(pn, jnp.sum)
                ps.append(pn.astype(BF16))
                vs.append(vt[hh, n])
            acc = acc + jnp.dot(jnp.concatenate(vs, axis=1), jnp.concatenate(ps, axis=0),
                                preferred_element_type=F32)
            out += [l_run, acc]
        return tuple(out)

    carry = lax.fori_loop(0, n_grp, values, tuple(carry0))
    for hh in heads:
        l_row = jnp.sum(carry[2 * hh], axis=0, keepdims=True)
        o_ref[:, hh * MOBA_HD:(hh + 1) * MOBA_HD] = (carry[2 * hh + 1] / l_row).T.astype(o_ref.dtype)


def _moba(z3, gq, gk):
    b_sz, s, _ = z3.shape
    n_blk = s // MOBA_BLOCK
    hd = MOBA_HD
    hps = MOBA_HPS

    def head_spec(rows, off, hh, row_index):
        return pl.BlockSpec((None, rows, hd), lambda b, h, i: (b, row_index(i), off // hd + h * hps + hh))

    q_specs = [head_spec(MOBA_BLOCK, OFF_MQ, hh, lambda i: i) for hh in range(hps)]
    k_specs = [head_spec(s, OFF_MK, hh, lambda i: 0) for hh in range(hps)]
    v_specs = [head_spec(s, OFF_MV, hh, lambda i: 0) for hh in range(hps)]
    vec = pl.BlockSpec((1, hd), lambda b, h, i: (0, 0))
    return pl.pallas_call(
        functools.partial(_moba_kernel, n_blk=n_blk),
        out_shape=jax.ShapeDtypeStruct((b_sz, s, MOBA_HEADS * hd), BF16),
        grid=(b_sz, MOBA_HEADS // hps, n_blk),
        in_specs=q_specs + k_specs + v_specs + [vec, vec],
        out_specs=pl.BlockSpec((None, MOBA_BLOCK, hps * hd), lambda b, h, i: (b, i, h)),
        scratch_shapes=[
            pltpu.VMEM((hps, s, hd), BF16),
            pltpu.VMEM((hps, n_blk, hd, MOBA_BLOCK), BF16),
            pltpu.VMEM((hps, n_blk, hd), F32),
            pltpu.VMEM((hps, n_blk, MOBA_BLOCK), F32),
            pltpu.VMEM((hps, n_blk, MOBA_BLOCK, MOBA_BLOCK), F32),
            pltpu.VMEM((hps, MOBA_BLOCK, MOBA_BLOCK), F32),
        ],
        compiler_params=_cparams(("arbitrary", "arbitrary", "arbitrary")),
        name="moba",
    )(*([z3] * (3 * hps)), gq, gk)


MERGE_COLS = 512


def _merge_kernel(a_ref, gl_ref, mo_ref, gt_ref, x_ref, wc_ref, wg_ref, wm_ref, wo_ref, gn_ref,
                  o_ref, hn_ref, mg):
    d = D_MODEL
    for c0 in range(0, d, MERGE_COLS):
        cs = slice(c0, c0 + MERGE_COLS)
        ya = jnp.dot(a_ref[...], wc_ref[:, cs], preferred_element_type=F32)
        yb = jnp.dot(gl_ref[...], wg_ref[:, cs], preferred_element_type=F32)
        yc = jnp.dot(mo_ref[...], wm_ref[:, cs], preferred_element_type=F32)
        g0 = gt_ref[:, c0:c0 + MERGE_COLS].astype(F32)
        g1 = gt_ref[:, d + c0:d + c0 + MERGE_COLS].astype(F32)
        g2 = gt_ref[:, 2 * d + c0:2 * d + c0 + MERGE_COLS].astype(F32)
        mg[:, cs] = (g0 * ya + g1 * yb + g2 * yc).astype(BF16)
    x_new = x_ref[...] + jnp.dot(mg[...], wo_ref[...], preferred_element_type=F32)
    o_ref[...] = x_new
    hn_ref[...] = _rms_rows(x_new, gn_ref[...]).astype(hn_ref.dtype)


def _resident(w_stack, layer):
    return pl.BlockSpec((None,) + w_stack.shape[1:], lambda i: (layer, 0, 0), pipeline_mode=pl.Buffered(1))


def _merge(a_act, o_gla, o_moba, gates, x, wc, wg, wm, wo, g_next, layer, *, tm=256):
    m, d = x.shape

    def rows(width):
        return pl.BlockSpec((tm, width), lambda i: (i, 0))

    return pl.pallas_call(
        _merge_kernel,
        out_shape=(jax.ShapeDtypeStruct((m, d), F32), jax.ShapeDtypeStruct((m, d), BF16)),
        grid=(m // tm,),
        in_specs=[rows(a_act.shape[1]), rows(o_gla.shape[1]), rows(o_moba.shape[1]), rows(gates.shape[1]),
                  rows(d), _resident(wc, layer), _resident(wg, layer), _resident(wm, layer),
                  _resident(wo, layer), _layer_vec(d, layer)],
        out_specs=(rows(d), rows(d)),
        scratch_shapes=[pltpu.VMEM((tm, d), BF16)],
        compiler_params=_cparams(("arbitrary",)),
        name="merge",
    )(a_act, o_gla, o_moba, gates, x, wc, wg, wm, wo, g_next)


def _shift_rows(u, k, carry):
    rolled = pltpu.roll(u, k, 0)
    head = jnp.where(lax.broadcasted_iota(jnp.int32, carry.shape, 0) < k,
                     pltpu.roll(carry, k, 0), rolled[:SUBLANE, :])
    return jnp.concatenate([head, rolled[SUBLANE:, :]], axis=0)


def _ffn_kernel(x_ref, h_ref, wug_ref, wuv_ref, cwg_ref, cwv_ref, wd_ref, gn_ref, o_ref, hn_ref, cg, cv,
                *, tiles_per_seq):
    i = pl.program_id(0)
    j = pl.program_id(1)
    tm = x_ref.shape[0]

    @pl.when(j == 0)
    def _():
        o_ref[...] = x_ref[...]

    @pl.when(i % tiles_per_seq == 0)
    def _():
        cg[j] = jnp.zeros(cg.shape[1:], F32)
        cv[j] = jnp.zeros(cv.shape[1:], F32)

    h = h_ref[...]

    def conv(w_ref, cw_ref, carry_ref):
        u = jnp.dot(h, w_ref[...], preferred_element_type=F32)
        carry = carry_ref[j]
        out = (u * cw_ref[2:3, :] + _shift_rows(u, 1, carry) * cw_ref[1:2, :]
               + _shift_rows(u, 2, carry) * cw_ref[0:1, :])
        carry_ref[j] = u[tm - SUBLANE:, :]
        return out

    ug = conv(wug_ref, cwg_ref, cg)
    uv = conv(wuv_ref, cwv_ref, cv)
    act = (_silu(ug) * uv).astype(BF16)
    o_ref[...] += jnp.dot(act, wd_ref[...], preferred_element_type=F32)

    @pl.when(j == pl.num_programs(1) - 1)
    def _():
        hn_ref[...] = _rms_rows(o_ref[...], gn_ref[...]).astype(hn_ref.dtype)


def _ffn(x, h, w_up, conv_w, w_down, g_next, layer, *, seq, tm=512, tf=512):
    m, d = x.shape
    f = w_down.shape[1]
    nf = f // tf
    return pl.pallas_call(
        functools.partial(_ffn_kernel, tiles_per_seq=seq // tm),
        out_shape=(jax.ShapeDtypeStruct((m, d), F32), jax.ShapeDtypeStruct((m, d), BF16)),
        grid=(m // tm, nf),
        in_specs=[
            pl.BlockSpec((tm, d), lambda i, j: (i, 0)),
            pl.BlockSpec((tm, d), lambda i, j: (i, 0)),
            pl.BlockSpec((None, d, tf), lambda i, j: (layer, 0, j)),
            pl.BlockSpec((None, d, tf), lambda i, j: (layer, 0, j + nf)),
            pl.BlockSpec((None, FFN_CONV_WIDTH, tf), lambda i, j: (layer, 0, j)),
            pl.BlockSpec((None, FFN_CONV_WIDTH, tf), lambda i, j: (layer, 0, j + nf)),
            pl.BlockSpec((None, tf, d), lambda i, j: (layer, j, 0)),
            pl.BlockSpec((None, 1, d), lambda i, j: (layer, 0, 0)),
        ],
        out_specs=(pl.BlockSpec((tm, d), lambda i, j: (i, 0)), pl.BlockSpec((tm, d), lambda i, j: (i, 0))),
        scratch_shapes=[
            pltpu.VMEM((nf, SUBLANE, tf), F32),
            pltpu.VMEM((nf, SUBLANE, tf), F32),
        ],
        compiler_params=_cparams(("arbitrary", "arbitrary")),
        name="conv_ffn",
    )(x, h, w_up, w_up, conv_w, conv_w, w_down, g_next)


PLE_COLS = 512


def _ple_kernel(x_ref, h_ref, p_ref, wg_ref, wp_ref, *rest):
    o_ref = rest[-2] if len(rest) == 3 else rest[-1]
    h = h_ref[...]
    p_bf = p_ref[...].astype(BF16)
    for c0 in range(0, D_MODEL, PLE_COLS):
        cs = slice(c0, c0 + PLE_COLS)
        gate = _sigmoid(jnp.dot(h, wg_ref[:, cs], preferred_element_type=F32))
        emb = jnp.dot(p_bf, wp_ref[:, cs], preferred_element_type=F32)
        o_ref[:, cs] = x_ref[:, cs] + gate * emb
    if len(rest) == 3:
        gn_ref, _, hn_ref = rest
        hn_ref[...] = _rms_rows(o_ref[...], gn_ref[...]).astype(hn_ref.dtype)


def _ple(x, h, p, w_gate, w_ple, layer, g_next, *, tm=512):
    m, d = x.shape
    rows = lambda width: pl.BlockSpec((tm, width), lambda i: (i, 0))
    in_specs = [rows(d), rows(d), pl.BlockSpec((None, tm, PLE_DIM), lambda i: (layer, i, 0)),
                _resident(w_gate, layer), _resident(w_ple, layer)]
    args = [x, h, p, w_gate, w_ple]
    out_shape = [jax.ShapeDtypeStruct((m, d), F32)]
    out_specs = [rows(d)]
    if g_next is not None:
        in_specs.append(_layer_vec(d, layer + 1))
        args.append(g_next)
        out_shape.append(jax.ShapeDtypeStruct((m, d), BF16))
        out_specs.append(rows(d))
    return pl.pallas_call(
        _ple_kernel,
        out_shape=tuple(out_shape),
        grid=(m // tm,),
        in_specs=in_specs,
        out_specs=tuple(out_specs),
        compiler_params=_cparams(("arbitrary",)),
        name="ple",
    )(*args)


def _pad_in_proj(w_in):
    ga_end = OFF_GA + GLA_GATE_RANK
    pad = jnp.zeros(w_in.shape[:-1] + (GATE_PAD - GLA_GATE_RANK,), w_in.dtype)
    return jnp.concatenate([w_in[..., :ga_end], pad, w_in[..., ga_end:]], axis=-1)


def kernel(x, p, norm_mix_g, w_in, conv_w, conv_b, conv_ln_g, conv_ln_b, w_conv_out, gla_w_a2, gla_b_a,
           gla_norm_g, w_gla_out, moba_q_norm_g, moba_k_norm_g, w_moba_out, w_gate, b_gate, w_out,
           norm_ffn_g, w_up, ffn_conv_w, w_down, norm_ple_g, w_ple_gate, w_ple):
    b_sz, s, d = x.shape
    depth = w_in.shape[0]
    m = b_sz * s
    xf = x.reshape(m, d)
    p2 = p.reshape(depth, m, PLE_DIM)
    row = lambda v: v.reshape(1, -1)
    stack_vec = lambda v: v.reshape(depth, 1, -1)
    w_in_p = _pad_in_proj(w_in).astype(BF16)
    w_gate_b, w_up_b, w_down_b = w_gate.astype(BF16), w_up.astype(BF16), w_down.astype(BF16)
    wc_b, wg_b, wm_b, wo_b = (w.astype(BF16) for w in (w_conv_out, w_gla_out, w_moba_out, w_out))
    w_pg_b, w_ple_b = w_ple_gate.astype(BF16), w_ple.astype(BF16)
    w_a2p = jnp.concatenate(
        [gla_w_a2, jnp.zeros((depth, GATE_PAD - GLA_GATE_RANK, GLA_QK), F32)], axis=1).astype(BF16)
    g_mix, g_ffn, g_ple, b_gate3 = (stack_vec(v) for v in (norm_mix_g, norm_ffn_g, norm_ple_g, b_gate))

    h = _rmsnorm(xf, g_mix, 0)
    for i in range(depth):
        z = _proj(h, w_in_p, None, i, tm=2048, tn=1152, name="in_proj")
        gates = _proj(h, w_gate_b, b_gate3, i, tm=2048, tn=1536, name="gate_proj")
        z3 = z.reshape(b_sz, s, Z_COLS)
        a_act = _conformer_conv(z3, conv_w[i], row(conv_b[i]), row(conv_ln_g[i]), row(conv_ln_b[i]))
        o_gla = _gla(z3, w_a2p[i], row(gla_b_a[i]), row(gla_norm_g[i]))
        o_moba = _moba(z3, row(moba_q_norm_g[i]), row(moba_k_norm_g[i]))
        xf, h = _merge(a_act.reshape(m, -1), o_gla.reshape(m, -1), o_moba.reshape(m, -1), gates, xf,
                       wc_b, wg_b, wm_b, wo_b, g_ffn, i)
        xf, h = _ffn(xf, h, w_up_b, ffn_conv_w, w_down_b, g_ple, i, seq=s)
        if i + 1 < depth:
            xf, h = _ple(xf, h, p2, w_pg_b, w_ple_b, i, g_mix)
        else:
            (xf,) = _ple(xf, h, p2, w_pg_b, w_ple_b, i, None)
    return xf.reshape(b_sz, s, d)
```

```python
import functools

import jax
import jax.numpy as jnp
from jax import lax
from jax.experimental import pallas as pl
from jax.experimental.pallas import tpu as pltpu

F32 = jnp.float32
BF16 = jnp.bfloat16

D_MODEL = 2048
PLE_DIM = 256
EPS = 1e-6
CONV_CH = 512
CONV_WIDTH = 31
GLA_HEADS = 4
GLA_DV = 128
GLA_DK = 64
GLA_GATE_RANK = 16
GLA_TAU = 16.0
GLA_CHUNK = 64
GLA_SUB = 16
MOBA_HEADS = 8
MOBA_HD = 128
MOBA_BLOCK = 256
MOBA_TOPK = 3
FFN_DIM = 5632
FFN_CONV_WIDTH = 3

LANE = 128
SUBLANE = 8
GATE_PAD = LANE
OFF_GLU = 0
OFF_GQ = OFF_GLU + 2 * CONV_CH
OFF_GK = OFF_GQ + GLA_HEADS * GLA_DK
OFF_GV = OFF_GK + GLA_HEADS * GLA_DK
OFF_GR = OFF_GV + GLA_HEADS * GLA_DV
OFF_GA = OFF_GR + GLA_HEADS * GLA_DV
OFF_MQ = OFF_GA + GATE_PAD
OFF_MK = OFF_MQ + MOBA_HEADS * MOBA_HD
OFF_MV = OFF_MK + MOBA_HEADS * MOBA_HD
Z_COLS = OFF_MV + MOBA_HEADS * MOBA_HD

VMEM_LIMIT = 56 * 1024 * 1024


def _cparams(sem):
    return pltpu.CompilerParams(dimension_semantics=sem, vmem_limit_bytes=VMEM_LIMIT)


def _sigmoid(x):
    return 1.0 / (1.0 + jnp.exp(-x))


def _silu(x):
    return x * _sigmoid(x)


def _rms_rows(x, g):
    ms = jnp.mean(x * x, axis=-1, keepdims=True)
    return x * lax.rsqrt(ms + EPS) * g


def _layer_vec(width, layer):
    return pl.BlockSpec((None, 1, width), lambda i: (layer, 0, 0))


def _rmsnorm_kernel(x_ref, g_ref, o_ref):
    o_ref[...] = _rms_rows(x_ref[...], g_ref[...]).astype(o_ref.dtype)


def _rmsnorm(x, g_stack, layer, *, tm=1024):
    m, d = x.shape
    return pl.pallas_call(
        _rmsnorm_kernel,
        out_shape=jax.ShapeDtypeStruct((m, d), BF16),
        grid=(m // tm,),
        in_specs=[pl.BlockSpec((tm, d), lambda i: (i, 0)), _layer_vec(d, layer)],
        out_specs=pl.BlockSpec((tm, d), lambda i: (i, 0)),
        compiler_params=_cparams(("arbitrary",)),
        name="rmsnorm",
    )(x, g_stack)


def _proj_kernel(h_ref, w_ref, o_ref):
    o_ref[...] = jnp.dot(h_ref[...], w_ref[...], preferred_element_type=F32).astype(o_ref.dtype)


def _proj_sigmoid_kernel(h_ref, w_ref, b_ref, o_ref):
    acc = jnp.dot(h_ref[...], w_ref[...].astype(BF16), preferred_element_type=F32)
    o_ref[...] = _sigmoid(acc + b_ref[...]).astype(o_ref.dtype)


def _proj(h, w_stack, b_stack, layer, *, tm, tn, name):
    m, d = h.shape
    n = w_stack.shape[2]
    in_specs = [pl.BlockSpec((tm, d), lambda i, j: (i, 0)),
                pl.BlockSpec((None, d, tn), lambda i, j: (layer, 0, j))]
    args = [h, w_stack]
    if b_stack is not None:
        in_specs.append(pl.BlockSpec((None, 1, tn), lambda i, j: (layer, 0, j)))
        args.append(b_stack)
    return pl.pallas_call(
        _proj_kernel if b_stack is None else _proj_sigmoid_kernel,
        out_shape=jax.ShapeDtypeStruct((m, n), BF16),
        grid=(m // tm, n // tn),
        in_specs=in_specs,
        out_specs=pl.BlockSpec((tm, tn), lambda i, j: (i, j)),
        compiler_params=_cparams(("arbitrary", "arbitrary")),
        name=name,
    )(*args)


CONV_HALO = 32
CONV_ROWS = 64


def _conv_kernel(z_ref, w_ref, cb_ref, lg_ref, lb_ref, o_ref, abuf, sh, *, ts):
    t = pl.program_id(1)

    @pl.when(t == 0)
    def _():
        abuf[0:CONV_HALO, :] = jnp.zeros((CONV_HALO, CONV_CH), F32)

    @pl.when(t > 0)
    def _():
        abuf[0:CONV_HALO, :] = abuf[ts:ts + CONV_HALO, :]

    z = z_ref[...].astype(F32)
    abuf[CONV_HALO:CONV_HALO + ts, :] = z[:, :CONV_CH] * _sigmoid(z[:, CONV_CH:])

    sh_rows = sh.shape[1]
    for b in range(1, SUBLANE):
        sh[b] = abuf[b:b + sh_rows, :]

    first = CONV_HALO - (CONV_WIDTH - 1)
    for r0 in range(0, ts, CONV_ROWS):
        acc = abuf[r0 + CONV_HALO:r0 + CONV_HALO + CONV_ROWS, :] * w_ref[CONV_WIDTH - 1:CONV_WIDTH, :]
        for j in range(CONV_WIDTH - 1):
            res = (first + j) % SUBLANE
            base = r0 + first + j - res
            rows = abuf[base:base + CONV_ROWS, :] if res == 0 else sh[res, base:base + CONV_ROWS, :]
            acc = acc + rows * w_ref[j:j + 1, :]
        acc = acc + cb_ref[...]
        mu = jnp.mean(acc, axis=-1, keepdims=True)
        xc = acc - mu
        var = jnp.mean(xc * xc, axis=-1, keepdims=True)
        y = xc * lax.rsqrt(var + EPS) * lg_ref[...] + lb_ref[...]
        o_ref[r0:r0 + CONV_ROWS, :] = _silu(y).astype(o_ref.dtype)


def _conformer_conv(z3, conv_w, conv_b, ln_g, ln_b, layer, *, ts=512):
    b_sz, s, _ = z3.shape

    def par(a):
        return pl.BlockSpec((None,) + a.shape[1:], lambda b, t: (layer, 0, 0))

    return pl.pallas_call(
        functools.partial(_conv_kernel, ts=ts),
        out_shape=jax.ShapeDtypeStruct((b_sz, s, CONV_CH), BF16),
        grid=(b_sz, s // ts),
        in_specs=[
            pl.BlockSpec((None, ts, 2 * CONV_CH), lambda b, t: (b, t, OFF_GLU // (2 * CONV_CH))),
            par(conv_w), par(conv_b), par(ln_g), par(ln_b),
        ],
        out_specs=pl.BlockSpec((None, ts, CONV_CH), lambda b, t: (b, t, 0)),
        scratch_shapes=[pltpu.VMEM((CONV_HALO + ts, CONV_CH), F32),
                        pltpu.VMEM((SUBLANE, CONV_HALO + ts - SUBLANE, CONV_CH), F32)],
        compiler_params=_cparams(("arbitrary", "arbitrary")),
        name="conformer_conv",
    )(z3, conv_w, conv_b, ln_g, ln_b)


GLA_QK = GLA_HEADS * GLA_DK
GLA_V = GLA_HEADS * GLA_DV


def _gla_kernel(q_ref, k_ref, v_ref, r_ref, al_ref, wa_ref, ba_ref, ng_ref, o_ref,
                st, qf, kf, vf, bcs, pbig, intra):
    c = GLA_CHUNK

    @pl.when(pl.program_id(1) == 0)
    def _():
        st[...] = jnp.zeros(st.shape, F32)

    xa = jnp.dot(al_ref[...], wa_ref[...], preferred_element_type=F32) + ba_ref[...]
    la = (jnp.minimum(xa, 0.0) - jnp.log(1.0 + jnp.exp(-jnp.abs(xa)))) * (1.0 / GLA_TAU)
    row = lax.broadcasted_iota(jnp.int32, (c, c), 0)
    col = lax.broadcasted_iota(jnp.int32, (c, c), 1)
    tri = jnp.where(row >= col, 1.0, 0.0).astype(BF16)
    la_hi = la.astype(BF16)
    la_lo = (la - la_hi.astype(F32)).astype(BF16)
    bc = (jnp.dot(tri, la_hi, preferred_element_type=F32)
          + jnp.dot(tri, la_lo, preferred_element_type=F32))

    q = q_ref[...].astype(F32) * (GLA_DK ** -0.5)
    k = k_ref[...].astype(F32)
    qf[...] = q
    kf[...] = k
    vf[...] = v_ref[...].astype(F32)
    bcs[...] = bc

    b_last = bc[c - 1:c, :]
    q_in = (q * jnp.exp(bc)).astype(BF16)
    k_dec = (k * jnp.exp(b_last - bc)).astype(BF16)
    e_last = jnp.exp(b_last)
    v_bf = v_ref[...]

    srow = lax.broadcasted_iota(jnp.int32, (GLA_QK, GLA_V), 0) // GLA_DK
    scol = lax.broadcasted_iota(jnp.int32, (GLA_QK, GLA_V), 1) // GLA_DV
    sel = jnp.where(srow == scol, 1.0, 0.0).astype(BF16)

    for j in range(c // GLA_SUB):
        t0 = j * GLA_SUB
        tj = c - t0
        q_rows = qf[t0:c, :]
        bc_rows = bcs[t0:c, :]
        t_loc = lax.broadcasted_iota(jnp.int32, (GLA_SUB, GLA_QK), 0)
        for sl in range(GLA_SUB):
            s = t0 + sl
            e = jnp.exp(jnp.minimum(bc_rows - bcs[s:s + 1, :], 0.0))
            p = (q_rows * kf[s:s + 1, :]) * e
            p_own = jnp.where(t_loc >= sl, p[:GLA_SUB, :], 0.0)
            p = p_own if tj == GLA_SUB else jnp.concatenate([p_own, p[GLA_SUB:, :]], axis=0)
            pbig[sl * tj:(sl + 1) * tj, :] = p.astype(BF16)
        r_all = jnp.dot(pbig[0:GLA_SUB * tj, :], sel, preferred_element_type=F32)
        acc = jnp.zeros((tj, GLA_V), F32)
        for sl in range(GLA_SUB):
            s = t0 + sl
            acc = acc + r_all[sl * tj:(sl + 1) * tj, :] * vf[s:s + 1, :]
        if j == 0:
            intra[...] = acc
        else:
            intra[t0:c, :] = intra[t0:c, :] + acc

    for h in range(GLA_HEADS):
        ks = slice(h * GLA_DK, (h + 1) * GLA_DK)
        vs = slice(h * GLA_DV, (h + 1) * GLA_DV)
        st_h = st[h]
        inter = lax.dot_general(q_in[:, ks], st_h.astype(BF16), (((1,), (1,)), ((), ())),
                                preferred_element_type=F32)
        upd = lax.dot_general(v_bf[:, vs], k_dec[:, ks], (((0,), (0,)), ((), ())),
                              preferred_element_type=F32)
        st[h] = st_h * e_last[:, ks] + upd
        o_h = inter + intra[:, vs]
        y = _rms_rows(o_h, ng_ref[...])
        o_ref[:, vs] = (y * _silu(r_ref[:, vs].astype(F32))).astype(o_ref.dtype)


def _gla(z3, w_a2p, b_a, norm_g):
    b_sz, s, _ = z3.shape
    c = GLA_CHUNK

    def zspec(width, off):
        return pl.BlockSpec((None, c, width), lambda b, t: (b, t, off // width))

    return pl.pallas_call(
        _gla_kernel,
        out_shape=jax.ShapeDtypeStruct((b_sz, s, GLA_V), BF16),
        grid=(b_sz, s // c),
        in_specs=[
            zspec(GLA_QK, OFF_GQ), zspec(GLA_QK, OFF_GK), zspec(GLA_V, OFF_GV), zspec(GLA_V, OFF_GR),
            zspec(GATE_PAD, OFF_GA),
            pl.BlockSpec((GATE_PAD, GLA_QK), lambda b, t: (0, 0)),
            pl.BlockSpec((1, GLA_QK), lambda b, t: (0, 0)),
            pl.BlockSpec((1, GLA_DV), lambda b, t: (0, 0)),
        ],
        out_specs=pl.BlockSpec((None, c, GLA_V), lambda b, t: (b, t, 0)),
        scratch_shapes=[
            pltpu.VMEM((GLA_HEADS, GLA_DV, GLA_DK), F32),
            pltpu.VMEM((c, GLA_QK), F32),
            pltpu.VMEM((c, GLA_QK), F32),
            pltpu.VMEM((c, GLA_V), F32),
            pltpu.VMEM((c, GLA_QK), F32),
            pltpu.VMEM((GLA_SUB * c, GLA_QK), BF16),
            pltpu.VMEM((c, GLA_V), F32),
        ],
        compiler_params=_cparams(("arbitrary", "arbitrary")),
        name="gla",
    )(z3, z3, z3, z3, z3, w_a2p, b_a, norm_g)


NEG_INF = float("-inf")
MOBA_GROUP = 4
NT_DIMS = (((1,), (1,)), ((), ()))


def _fold_rows(x, op):
    return op(x.reshape(x.shape[0] // SUBLANE, SUBLANE, x.shape[1]), axis=0)


MOBA_HPS = 2


def _moba_kernel(*refs, n_blk):
    hps = MOBA_HPS
    q_refs, k_refs, v_refs = refs[:hps], refs[hps:2 * hps], refs[2 * hps:3 * hps]
    gq_ref, gk_ref, o_ref, kn, vt, kmean, chosen, s_scr, s_own = refs[3 * hps:]
    qi = pl.program_id(2)
    blk = MOBA_BLOCK
    grp = MOBA_GROUP
    heads = range(hps)

    @pl.when(qi == 0)
    def _():
        for hh in heads:
            for n in range(n_blk):
                kb = _rms_rows(k_refs[hh][n * blk:(n + 1) * blk, :].astype(F32), gk_ref[...])
                kn[hh, n * blk:(n + 1) * blk, :] = kb.astype(BF16)
                kmean[hh, n:n + 1, :] = jnp.mean(kb, axis=0, keepdims=True)
                vt[hh, n] = v_refs[hh][n * blk:(n + 1) * blk, :].astype(F32).T.astype(BF16)

    brow = lax.broadcasted_iota(jnp.int32, (n_blk, blk), 0)
    past = brow < qi
    k_i = lax.broadcasted_iota(jnp.int32, (blk, blk), 0)
    q_i = lax.broadcasted_iota(jnp.int32, (blk, blk), 1)
    own = pl.multiple_of(qi * blk, blk)
    q_bf, m_run0 = [], []
    for hh in heads:
        qn = _rms_rows(q_refs[hh][...].astype(F32), gq_ref[...])
        q_bf.append((qn * (MOBA_HD ** -0.5)).astype(BF16))
        sb = lax.dot_general(kmean[hh], qn, NT_DIMS, precision=lax.Precision.HIGHEST,
                             preferred_element_type=F32)
        sb = jnp.where(past, sb, NEG_INF)
        rank = jnp.zeros((n_blk, blk), F32)
        for m in range(n_blk):
            cm = sb[m:m + 1, :]
            ge = jnp.where(cm >= sb, 1.0, 0.0)
            gt = jnp.where(cm > sb, 1.0, 0.0)
            rank = rank + jnp.where(brow > m, ge, gt)
        chosen[hh] = jnp.where(past, jnp.where(rank < float(MOBA_TOPK), 1.0, 0.0), 0.0)
        so = lax.dot_general(kn[hh, pl.ds(own, blk), :], q_bf[hh], NT_DIMS, preferred_element_type=F32)
        so = jnp.where(k_i <= q_i, so, NEG_INF)
        s_own[hh] = so
        m_run0.append(_fold_rows(so, jnp.max))
    n_grp = (qi + grp - 1) // grp

    def scores(g, m_runs):
        start = pl.multiple_of(g * (grp * blk), grp * blk)
        out = []
        for hh in heads:
            m_run = m_runs[hh]
            sg = lax.dot_general(kn[hh, pl.ds(start, grp * blk), :], q_bf[hh], NT_DIMS,
                                 preferred_element_type=F32)
            for b in range(grp):
                n = g * grp + b
                sn = jnp.where(chosen[hh, pl.ds(n, 1), :] > 0.0, sg[b * blk:(b + 1) * blk, :], NEG_INF)
                s_scr[hh, n] = sn
                m_run = jnp.maximum(m_run, _fold_rows(sn, jnp.max))
            out.append(m_run)
        return tuple(out)

    m_runs = lax.fori_loop(0, n_grp, scores, tuple(m_run0))
    m_rows = [jnp.max(m_runs[hh], axis=0, keepdims=True) for hh in heads]

    carry0 = []
    for hh in heads:
        p_own = jnp.exp(s_own[hh] - m_rows[hh])
        carry0.append(_fold_rows(p_own, jnp.sum))
        carry0.append(jnp.dot(vt[hh, qi], p_own.astype(BF16), preferred_element_type=F32))

    def values(g, carry):
        out = []
        for hh in heads:
            l_run, acc = carry[2 * hh], carry[2 * hh + 1]
            ps, vs = [], []
            for b in range(grp):
                n = g * grp + b
                pn = jnp.exp(s_scr[hh, n] - m_rows[hh])
                l_run = l_run + _fold_rows(pn, jnp.sum)
                ps.append(pn.astype(BF16))
                vs.append(vt[hh, n])
            acc = acc + jnp.dot(jnp.concatenate(vs, axis=1), jnp.concatenate(ps, axis=0),
                                preferred_element_type=F32)
            out += [l_run, acc]
        return tuple(out)

    carry = lax.fori_loop(0, n_grp, values, tuple(carry0))
    for hh in heads:
        l_row = jnp.sum(carry[2 * hh], axis=0, keepdims=True)
        o_ref[:, hh * MOBA_HD:(hh + 1) * MOBA_HD] = (carry[2 * hh + 1] / l_row).T.astype(o_ref.dtype)


def _moba(z3, gq, gk):
    b_sz, s, _ = z3.shape
    n_blk = s // MOBA_BLOCK
    hd = MOBA_HD
    hps = MOBA_HPS

    def head_spec(rows, off, hh, row_index):
        return pl.BlockSpec((None, rows, hd), lambda b, h, i: (b, row_index(i), off // hd + h * hps + hh))

    q_specs = [head_spec(MOBA_BLOCK, OFF_MQ, hh, lambda i: i) for hh in range(hps)]
    k_specs = [head_spec(s, OFF_MK, hh, lambda i: 0) for hh in range(hps)]
    v_specs = [head_spec(s, OFF_MV, hh, lambda i: 0) for hh in range(hps)]
    vec = pl.BlockSpec((1, hd), lambda b, h, i: (0, 0))
    return pl.pallas_call(
        functools.partial(_moba_kernel, n_blk=n_blk),
        out_shape=jax.ShapeDtypeStruct((b_sz, s, MOBA_HEADS * hd), BF16),
        grid=(b_sz, MOBA_HEADS // hps, n_blk),
        in_specs=q_specs + k_specs + v_specs + [vec, vec],
        out_specs=pl.BlockSpec((None, MOBA_BLOCK, hps * hd), lambda b, h, i: (b, i, h)),
        scratch_shapes=[
            pltpu.VMEM((hps, s, hd), BF16),
            pltpu.VMEM((hps, n_blk, hd, MOBA_BLOCK), BF16),
            pltpu.VMEM((hps, n_blk, hd), F32),
            pltpu.VMEM((hps, n_blk, MOBA_BLOCK), F32),
            pltpu.VMEM((hps, n_blk, MOBA_BLOCK, MOBA_BLOCK), F32),
            pltpu.VMEM((hps, MOBA_BLOCK, MOBA_BLOCK), F32),
        ],
        compiler_params=_cparams(("arbitrary", "arbitrary", "arbitrary")),
        name="moba",
    )(*([z3] * (3 * hps)), gq, gk)


MERGE_COLS = 512


def _merge_kernel(a_ref, gl_ref, mo_ref, gt_ref, x_ref, wc_ref, wg_ref, wm_ref, wo_ref, gn_ref,
                  o_ref, hn_ref, mg):
    d = D_MODEL
    for c0 in range(0, d, MERGE_COLS):
        cs = slice(c0, c0 + MERGE_COLS)
        ya = jnp.dot(a_ref[...], wc_ref[:, cs], preferred_element_type=F32)
        yb = jnp.dot(gl_ref[...], wg_ref[:, cs], preferred_element_type=F32)
        yc = jnp.dot(mo_ref[...], wm_ref[:, cs], preferred_element_type=F32)
        g0 = gt_ref[:, c0:c0 + MERGE_COLS].astype(F32)
        g1 = gt_ref[:, d + c0:d + c0 + MERGE_COLS].astype(F32)
        g2 = gt_ref[:, 2 * d + c0:2 * d + c0 + MERGE_COLS].astype(F32)
        mg[:, cs] = (g0 * ya + g1 * yb + g2 * yc).astype(BF16)
    x_new = x_ref[...] + jnp.dot(mg[...], wo_ref[...], preferred_element_type=F32)
    o_ref[...] = x_new
    hn_ref[...] = _rms_rows(x_new, gn_ref[...]).astype(hn_ref.dtype)


def _resident(w_stack, layer):
    return pl.BlockSpec((None,) + w_stack.shape[1:], lambda i: (layer, 0, 0), pipeline_mode=pl.Buffered(1))


def _merge(a_act, o_gla, o_moba, gates, x, wc, wg, wm, wo, g_next, layer, *, tm=256):
    m, d = x.shape

    def rows(width):
        return pl.BlockSpec((tm, width), lambda i: (i, 0))

    return pl.pallas_call(
        _merge_kernel,
        out_shape=(jax.ShapeDtypeStruct((m, d), F32), jax.ShapeDtypeStruct((m, d), BF16)),
        grid=(m // tm,),
        in_specs=[rows(a_act.shape[1]), rows(o_gla.shape[1]), rows(o_moba.shape[1]), rows(gates.shape[1]),
                  rows(d), _resident(wc, layer), _resident(wg, layer), _resident(wm, layer),
                  _resident(wo, layer), _layer_vec(d, layer)],
        out_specs=(rows(d), rows(d)),
        scratch_shapes=[pltpu.VMEM((tm, d), BF16)],
        compiler_params=_cparams(("arbitrary",)),
        name="merge",
    )(a_act, o_gla, o_moba, gates, x, wc, wg, wm, wo, g_next)


def _shift_rows(u, k, carry):
    rolled = pltpu.roll(u, k, 0)
    head = jnp.where(lax.broadcasted_iota(jnp.int32, carry.shape, 0) < k,
                     pltpu.roll(carry, k, 0), rolled[:SUBLANE, :])
    return jnp.concatenate([head, rolled[SUBLANE:, :]], axis=0)


def _ffn_kernel(x_ref, h_ref, wug_ref, wuv_ref, cwg_ref, cwv_ref, wd_ref, gn_ref, o_ref, hn_ref, cg, cv,
                *, tiles_per_seq):
    i = pl.program_id(0)
    j = pl.program_id(1)
    tm = x_ref.shape[0]

    @pl.when(j == 0)
    def _():
        o_ref[...] = x_ref[...]

    @pl.when(i % tiles_per_seq == 0)
    def _():
        cg[j] = jnp.zeros(cg.shape[1:], F32)
        cv[j] = jnp.zeros(cv.shape[1:], F32)

    h = h_ref[...]

    def conv(w_ref, cw_ref, carry_ref):
        u = jnp.dot(h, w_ref[...], preferred_element_type=F32)
        carry = carry_ref[j]
        out = (u * cw_ref[2:3, :] + _shift_rows(u, 1, carry) * cw_ref[1:2, :]
               + _shift_rows(u, 2, carry) * cw_ref[0:1, :])
        carry_ref[j] = u[tm - SUBLANE:, :]
        return out

    ug = conv(wug_ref, cwg_ref, cg)
    uv = conv(wuv_ref, cwv_ref, cv)
    act = (_silu(ug) * uv).astype(BF16)
    o_ref[...] += jnp.dot(act, wd_ref[...], preferred_element_type=F32)

    @pl.when(j == pl.num_programs(1) - 1)
    def _():
        hn_ref[...] = _rms_rows(o_ref[...], gn_ref[...]).astype(hn_ref.dtype)


def _ffn(x, h, w_up, conv_w, w_down, g_next, layer, *, seq, tm=512, tf=512):
    m, d = x.shape
    f = w_down.shape[1]
    nf = f // tf
    return pl.pallas_call(
        functools.partial(_ffn_kernel, tiles_per_seq=seq // tm),
        out_shape=(jax.ShapeDtypeStruct((m, d), F32), jax.ShapeDtypeStruct((m, d), BF16)),
        grid=(m // tm, nf),
        in_specs=[
            pl.BlockSpec((tm, d), lambda i, j: (i, 0)),
            pl.BlockSpec((tm, d), lambda i, j: (i, 0)),
            pl.BlockSpec((None, d, tf), lambda i, j: (layer, 0, j)),
            pl.BlockSpec((None, d, tf), lambda i, j: (layer, 0, j + nf)),
            pl.BlockSpec((None, FFN_CONV_WIDTH, tf), lambda i, j: (layer, 0, j)),
            pl.BlockSpec((None, FFN_CONV_WIDTH, tf), lambda i, j: (layer, 0, j + nf)),
            pl.BlockSpec((None, tf, d), lambda i, j: (layer, j, 0)),
            pl.BlockSpec((None, 1, d), lambda i, j: (layer, 0, 0)),
        ],
        out_specs=(pl.BlockSpec((tm, d), lambda i, j: (i, 0)), pl.BlockSpec((tm, d), lambda i, j: (i, 0))),
        scratch_shapes=[
            pltpu.VMEM((nf, SUBLANE, tf), F32),
            pltpu.VMEM((nf, SUBLANE, tf), F32),
        ],
        compiler_params=_cparams(("arbitrary", "arbitrary")),
        name="conv_ffn",
    )(x, h, w_up, w_up, conv_w, conv_w, w_down, g_next)


PLE_COLS = 512


def _ple_kernel(x_ref, h_ref, p_ref, wg_ref, wp_ref, *rest):
    o_ref = rest[-2] if len(rest) == 3 else rest[-1]
    h = h_ref[...]
    p_bf = p_ref[...].astype(BF16)
    for c0 in range(0, D_MODEL, PLE_COLS):
        cs = slice(c0, c0 + PLE_COLS)
        gate = _sigmoid(jnp.dot(h, wg_ref[:, cs], preferred_element_type=F32))
        emb = jnp.dot(p_bf, wp_ref[:, cs], preferred_element_type=F32)
        o_ref[:, cs] = x_ref[:, cs] + gate * emb
    if len(rest) == 3:
        gn_ref, _, hn_ref = rest
        hn_ref[...] = _rms_rows(o_ref[...], gn_ref[...]).astype(hn_ref.dtype)


def _ple(x, h, p, w_gate, w_ple, layer, g_next, *, tm=512):
    m, d = x.shape
    rows = lambda width: pl.BlockSpec((tm, width), lambda i: (i, 0))
    in_specs = [rows(d), rows(d), pl.BlockSpec((None, tm, PLE_DIM), lambda i: (layer, i, 0)),
                _resident(w_gate, layer), _resident(w_ple, layer)]
    args = [x, h, p, w_gate, w_ple]
    out_shape = [jax.ShapeDtypeStruct((m, d), F32)]
    out_specs = [rows(d)]
    if g_next is not None:
        in_specs.append(_layer_vec(d, layer + 1))
        args.append(g_next)
        out_shape.append(jax.ShapeDtypeStruct((m, d), BF16))
        out_specs.append(rows(d))
    return pl.pallas_call(
        _ple_kernel,
        out_shape=tuple(out_shape),
        grid=(m // tm,),
        in_specs=in_specs,
        out_specs=tuple(out_specs),
        compiler_params=_cparams(("arbitrary",)),
        name="ple",
    )(*args)


def _pad_in_proj(w_in):
    ga_end = OFF_GA + GLA_GATE_RANK
    pad = jnp.zeros(w_in.shape[:-1] + (GATE_PAD - GLA_GATE_RANK,), BF16)
    return jnp.concatenate([w_in[..., :ga_end].astype(BF16), pad, w_in[..., ga_end:].astype(BF16)], axis=-1)


def kernel(x, p, norm_mix_g, w_in, conv_w, conv_b, conv_ln_g, conv_ln_b, w_conv_out, gla_w_a2, gla_b_a,
           gla_norm_g, w_gla_out, moba_q_norm_g, moba_k_norm_g, w_moba_out, w_gate, b_gate, w_out,
           norm_ffn_g, w_up, ffn_conv_w, w_down, norm_ple_g, w_ple_gate, w_ple):
    b_sz, s, d = x.shape
    depth = w_in.shape[0]
    m = b_sz * s
    xf = x.reshape(m, d)
    p2 = p.reshape(depth, m, PLE_DIM)
    row = lambda v: v.reshape(1, -1)
    stack_vec = lambda v: v.reshape(depth, 1, -1)
    w_in_p = _pad_in_proj(w_in)
    w_up_b, w_down_b = w_up.astype(BF16), w_down.astype(BF16)
    wc_b, wg_b, wm_b, wo_b = (w.astype(BF16) for w in (w_conv_out, w_gla_out, w_moba_out, w_out))
    w_pg_b, w_ple_b = w_ple_gate.astype(BF16), w_ple.astype(BF16)
    w_a2p = jnp.concatenate(
        [gla_w_a2, jnp.zeros((depth, GATE_PAD - GLA_GATE_RANK, GLA_QK), F32)], axis=1).astype(BF16)
    g_mix, g_ffn, g_ple, b_gate3 = (stack_vec(v) for v in (norm_mix_g, norm_ffn_g, norm_ple_g, b_gate))
    conv_b3, conv_ln_g3, conv_ln_b3 = (stack_vec(v) for v in (conv_b, conv_ln_g, conv_ln_b))

    h = _rmsnorm(xf, g_mix, 0)
    for i in range(depth):
        z = _proj(h, w_in_p, None, i, tm=2048, tn=1152, name="in_proj")
        gates = _proj(h, w_gate, b_gate3, i, tm=2048, tn=768, name="gate_proj")
        z3 = z.reshape(b_sz, s, Z_COLS)
        a_act = _conformer_conv(z3, conv_w, conv_b3, conv_ln_g3, conv_ln_b3, i)
        o_gla = _gla(z3, w_a2p[i], row(gla_b_a[i]), row(gla_norm_g[i]))
        o_moba = _moba(z3, row(moba_q_norm_g[i]), row(moba_k_norm_g[i]))
        xf, h = _merge(a_act.reshape(m, -1), o_gla.reshape(m, -1), o_moba.reshape(m, -1), gates, xf,
                       wc_b, wg_b, wm_b, wo_b, g_ffn, i)
        xf, h = _ffn(xf, h, w_up_b, ffn_conv_w, w_down_b, g_ple, i, seq=s)
        if i + 1 < depth:
            xf, h = _ple(xf, h, p2, w_pg_b, w_ple_b, i, g_mix)
        else:
            (xf,) = _ple(xf, h, p2, w_pg_b, w_ple_b, i, None)
    return xf.reshape(b_sz, s, d)
```

```python
import functools

import jax
import jax.numpy as jnp
from jax import lax
from jax.experimental import pallas as pl
from jax.experimental.pallas import tpu as pltpu

F32 = jnp.float32
BF16 = jnp.bfloat16

D_MODEL = 2048
PLE_DIM = 256
EPS = 1e-6
CONV_CH = 512
CONV_WIDTH = 31
GLA_HEADS = 4
GLA_DV = 128
GLA_DK = 64
GLA_GATE_RANK = 16
GLA_TAU = 16.0
GLA_CHUNK = 64
GLA_SUB = 16
MOBA_HEADS = 8
MOBA_HD = 128
MOBA_BLOCK = 256
MOBA_TOPK = 3
FFN_DIM = 5632
FFN_CONV_WIDTH = 3

LANE = 128
SUBLANE = 8
GATE_PAD = LANE
OFF_GLU = 0
OFF_GQ = OFF_GLU + 2 * CONV_CH
OFF_GK = OFF_GQ + GLA_HEADS * GLA_DK
OFF_GV = OFF_GK + GLA_HEADS * GLA_DK
OFF_GR = OFF_GV + GLA_HEADS * GLA_DV
OFF_GA = OFF_GR + GLA_HEADS * GLA_DV
OFF_MQ = OFF_GA + GATE_PAD
OFF_MK = OFF_MQ + MOBA_HEADS * MOBA_HD
OFF_MV = OFF_MK + MOBA_HEADS * MOBA_HD
Z_COLS = OFF_MV + MOBA_HEADS * MOBA_HD

VMEM_LIMIT = 56 * 1024 * 1024


def _cparams(sem):
    return pltpu.CompilerParams(dimension_semantics=sem, vmem_limit_bytes=VMEM_LIMIT)


def _sigmoid(x):
    return 1.0 / (1.0 + jnp.exp(-x))


def _silu(x):
    return x * _sigmoid(x)


def _rms_rows(x, g):
    ms = jnp.mean(x * x, axis=-1, keepdims=True)
    return x * lax.rsqrt(ms + EPS) * g


def _layer_vec(width, layer):
    return pl.BlockSpec((None, 1, width), lambda i: (layer, 0, 0))


def _rmsnorm_kernel(x_ref, g_ref, o_ref):
    o_ref[...] = _rms_rows(x_ref[...], g_ref[...]).astype(o_ref.dtype)


def _rmsnorm(x, g_stack, layer, *, tm=1024):
    m, d = x.shape
    return pl.pallas_call(
        _rmsnorm_kernel,
        out_shape=jax.ShapeDtypeStruct((m, d), BF16),
        grid=(m // tm,),
        in_specs=[pl.BlockSpec((tm, d), lambda i: (i, 0)), _layer_vec(d, layer)],
        out_specs=pl.BlockSpec((tm, d), lambda i: (i, 0)),
        compiler_params=_cparams(("arbitrary",)),
        name="rmsnorm",
    )(x, g_stack)


def _proj_kernel(h_ref, w_ref, o_ref):
    o_ref[...] = jnp.dot(h_ref[...], w_ref[...], preferred_element_type=F32).astype(o_ref.dtype)


def _proj_sigmoid_kernel(h_ref, w_ref, b_ref, o_ref):
    acc = jnp.dot(h_ref[...], w_ref[...].astype(BF16), preferred_element_type=F32)
    o_ref[...] = _sigmoid(acc + b_ref[...]).astype(o_ref.dtype)


def _proj(h, w_stack, b_stack, layer, *, tm, tn, name):
    m, d = h.shape
    n = w_stack.shape[2]
    in_specs = [pl.BlockSpec((tm, d), lambda i, j: (i, 0)),
                pl.BlockSpec((None, d, tn), lambda i, j: (layer, 0, j))]
    args = [h, w_stack]
    if b_stack is not None:
        in_specs.append(pl.BlockSpec((None, 1, tn), lambda i, j: (layer, 0, j)))
        args.append(b_stack)
    return pl.pallas_call(
        _proj_kernel if b_stack is None else _proj_sigmoid_kernel,
        out_shape=jax.ShapeDtypeStruct((m, n), BF16),
        grid=(m // tm, n // tn),
        in_specs=in_specs,
        out_specs=pl.BlockSpec((tm, tn), lambda i, j: (i, j)),
        compiler_params=_cparams(("arbitrary", "arbitrary")),
        name=name,
    )(*args)


CONV_HALO = 32
CONV_ROWS = 64


def _conv_kernel(z_ref, w_ref, cb_ref, lg_ref, lb_ref, o_ref, abuf, sh, *, ts):
    t = pl.program_id(1)

    @pl.when(t == 0)
    def _():
        abuf[0:CONV_HALO, :] = jnp.zeros((CONV_HALO, CONV_CH), F32)

    @pl.when(t > 0)
    def _():
        abuf[0:CONV_HALO, :] = abuf[ts:ts + CONV_HALO, :]

    z = z_ref[...].astype(F32)
    abuf[CONV_HALO:CONV_HALO + ts, :] = z[:, :CONV_CH] * _sigmoid(z[:, CONV_CH:])

    sh_rows = sh.shape[1]
    for b in range(1, SUBLANE):
        sh[b] = abuf[b:b + sh_rows, :]

    first = CONV_HALO - (CONV_WIDTH - 1)
    for r0 in range(0, ts, CONV_ROWS):
        acc = abuf[r0 + CONV_HALO:r0 + CONV_HALO + CONV_ROWS, :] * w_ref[CONV_WIDTH - 1:CONV_WIDTH, :]
        for j in range(CONV_WIDTH - 1):
            res = (first + j) % SUBLANE
            base = r0 + first + j - res
            rows = abuf[base:base + CONV_ROWS, :] if res == 0 else sh[res, base:base + CONV_ROWS, :]
            acc = acc + rows * w_ref[j:j + 1, :]
        acc = acc + cb_ref[...]
        mu = jnp.mean(acc, axis=-1, keepdims=True)
        xc = acc - mu
        var = jnp.mean(xc * xc, axis=-1, keepdims=True)
        y = xc * lax.rsqrt(var + EPS) * lg_ref[...] + lb_ref[...]
        o_ref[r0:r0 + CONV_ROWS, :] = _silu(y).astype(o_ref.dtype)


def _conformer_conv(z3, conv_w, conv_b, ln_g, ln_b, layer, *, ts=512):
    b_sz, s, _ = z3.shape

    def par(a):
        return pl.BlockSpec((None,) + a.shape[1:], lambda b, t: (layer, 0, 0))

    return pl.pallas_call(
        functools.partial(_conv_kernel, ts=ts),
        out_shape=jax.ShapeDtypeStruct((b_sz, s, CONV_CH), BF16),
        grid=(b_sz, s // ts),
        in_specs=[
            pl.BlockSpec((None, ts, 2 * CONV_CH), lambda b, t: (b, t, OFF_GLU // (2 * CONV_CH))),
            par(conv_w), par(conv_b), par(ln_g), par(ln_b),
        ],
        out_specs=pl.BlockSpec((None, ts, CONV_CH), lambda b, t: (b, t, 0)),
        scratch_shapes=[pltpu.VMEM((CONV_HALO + ts, CONV_CH), F32),
                        pltpu.VMEM((SUBLANE, CONV_HALO + ts - SUBLANE, CONV_CH), F32)],
        compiler_params=_cparams(("arbitrary", "arbitrary")),
        name="conformer_conv",
    )(z3, conv_w, conv_b, ln_g, ln_b)


LOG2_E = 1.4426950408889634
GLA_QK = GLA_HEADS * GLA_DK
GLA_V = GLA_HEADS * GLA_DV


GLA_CPS = 4


def _gla_kernel(q_ref, k_ref, v_ref, r_ref, al_ref, wa_ref, ba_ref, ng_ref, o_ref,
                st, qf, kf, vf, bcs, pbig, intra):
    @pl.when(pl.program_id(1) == 0)
    def _():
        st[...] = jnp.zeros(st.shape, F32)

    for ci in range(GLA_CPS):
        rs = slice(ci * GLA_CHUNK, (ci + 1) * GLA_CHUNK)
        _gla_chunk(q_ref.at[rs, :], k_ref.at[rs, :], v_ref.at[rs, :], r_ref.at[rs, :], al_ref.at[rs, :],
                   wa_ref, ba_ref, ng_ref, o_ref.at[rs, :],
                   st, qf.at[ci], kf.at[ci], vf.at[ci], bcs.at[ci], pbig.at[ci], intra.at[ci])


def _gla_chunk(q_ref, k_ref, v_ref, r_ref, al_ref, wa_ref, ba_ref, ng_ref, o_ref,
               st, qf, kf, vf, bcs, pbig, intra):
    c = GLA_CHUNK

    xa = jnp.dot(al_ref[...], wa_ref[...], preferred_element_type=F32) + ba_ref[...]
    la = (jnp.minimum(xa, 0.0) - jnp.log(1.0 + jnp.exp(-jnp.abs(xa)))) * (1.0 / GLA_TAU)
    row = lax.broadcasted_iota(jnp.int32, (c, c), 0)
    col = lax.broadcasted_iota(jnp.int32, (c, c), 1)
    tri = jnp.where(row >= col, 1.0, 0.0).astype(BF16)
    la_hi = la.astype(BF16)
    la_lo = (la - la_hi.astype(F32)).astype(BF16)
    bc = (jnp.dot(tri, la_hi, preferred_element_type=F32)
          + jnp.dot(tri, la_lo, preferred_element_type=F32))

    q = q_ref[...].astype(F32) * (GLA_DK ** -0.5)
    k = k_ref[...].astype(F32)
    qf[...] = q
    kf[...] = k
    vf[...] = v_ref[...].astype(F32)
    bcs[...] = bc * LOG2_E

    b_last = bc[c - 1:c, :]
    q_in = (q * jnp.exp(bc)).astype(BF16)
    k_dec = (k * jnp.exp(b_last - bc)).astype(BF16)
    e_last = jnp.exp(b_last)
    v_bf = v_ref[...]

    srow = lax.broadcasted_iota(jnp.int32, (GLA_QK, GLA_V), 0) // GLA_DK
    scol = lax.broadcasted_iota(jnp.int32, (GLA_QK, GLA_V), 1) // GLA_DV
    sel = jnp.where(srow == scol, 1.0, 0.0).astype(BF16)

    for j in range(c // GLA_SUB):
        t0 = j * GLA_SUB
        tj = c - t0
        q_rows = qf[t0:c, :]
        bc_rows = bcs[t0:c, :]
        t_loc = lax.broadcasted_iota(jnp.int32, (GLA_SUB, GLA_QK), 0)
        for sl in range(GLA_SUB):
            s = t0 + sl
            d = bc_rows - bcs[s:s + 1, :]
            d_own = jnp.minimum(d[:GLA_SUB, :], 0.0)
            d = d_own if tj == GLA_SUB else jnp.concatenate([d_own, d[GLA_SUB:, :]], axis=0)
            p = (q_rows * kf[s:s + 1, :]) * jnp.exp2(d)
            p_own = jnp.where(t_loc >= sl, p[:GLA_SUB, :], 0.0)
            p = p_own if tj == GLA_SUB else jnp.concatenate([p_own, p[GLA_SUB:, :]], axis=0)
            pbig[sl * tj:(sl + 1) * tj, :] = p.astype(BF16)
        r_all = jnp.dot(pbig[0:GLA_SUB * tj, :], sel, preferred_element_type=F32)
        acc = jnp.zeros((tj, GLA_V), F32)
        for sl in range(GLA_SUB):
            s = t0 + sl
            acc = acc + r_all[sl * tj:(sl + 1) * tj, :] * vf[s:s + 1, :]
        if j == 0:
            intra[...] = acc
        else:
            intra[t0:c, :] = intra[t0:c, :] + acc

    for h in range(GLA_HEADS):
        ks = slice(h * GLA_DK, (h + 1) * GLA_DK)
        vs = slice(h * GLA_DV, (h + 1) * GLA_DV)
        st_h = st[h]
        inter = lax.dot_general(q_in[:, ks], st_h.astype(BF16), (((1,), (1,)), ((), ())),
                                preferred_element_type=F32)
        upd = lax.dot_general(v_bf[:, vs], k_dec[:, ks], (((0,), (0,)), ((), ())),
                              preferred_element_type=F32)
        st[h] = st_h * e_last[:, ks] + upd
        o_h = inter + intra[:, vs]
        y = _rms_rows(o_h, ng_ref[...])
        o_ref[:, vs] = (y * _silu(r_ref[:, vs].astype(F32))).astype(o_ref.dtype)


def _gla(z3, w_a2p, b_a, norm_g):
    b_sz, s, _ = z3.shape
    c = GLA_CHUNK
    rows = c * GLA_CPS
    cps = GLA_CPS

    def zspec(width, off):
        return pl.BlockSpec((None, rows, width), lambda b, t: (b, t, off // width))

    return pl.pallas_call(
        _gla_kernel,
        out_shape=jax.ShapeDtypeStruct((b_sz, s, GLA_V), BF16),
        grid=(b_sz, s // rows),
        in_specs=[
            zspec(GLA_QK, OFF_GQ), zspec(GLA_QK, OFF_GK), zspec(GLA_V, OFF_GV), zspec(GLA_V, OFF_GR),
            zspec(GATE_PAD, OFF_GA),
            pl.BlockSpec((GATE_PAD, GLA_QK), lambda b, t: (0, 0)),
            pl.BlockSpec((1, GLA_QK), lambda b, t: (0, 0)),
            pl.BlockSpec((1, GLA_DV), lambda b, t: (0, 0)),
        ],
        out_specs=pl.BlockSpec((None, rows, GLA_V), lambda b, t: (b, t, 0)),
        scratch_shapes=[
            pltpu.VMEM((GLA_HEADS, GLA_DV, GLA_DK), F32),
            pltpu.VMEM((cps, c, GLA_QK), F32),
            pltpu.VMEM((cps, c, GLA_QK), F32),
            pltpu.VMEM((cps, c, GLA_V), F32),
            pltpu.VMEM((cps, c, GLA_QK), F32),
            pltpu.VMEM((cps, GLA_SUB * c, GLA_QK), BF16),
            pltpu.VMEM((cps, c, GLA_V), F32),
        ],
        compiler_params=_cparams(("arbitrary", "arbitrary")),
        name="gla",
    )(z3, z3, z3, z3, z3, w_a2p, b_a, norm_g)


NEG_INF = float("-inf")
MOBA_GROUP = 4
NT_DIMS = (((1,), (1,)), ((), ()))


def _fold_rows(x, op):
    return op(x.reshape(x.shape[0] // SUBLANE, SUBLANE, x.shape[1]), axis=0)


MOBA_HPS = 4


def _moba_kernel(*refs, n_blk):
    hps = MOBA_HPS
    q_refs, k_refs, v_refs = refs[:hps], refs[hps:2 * hps], refs[2 * hps:3 * hps]
    gq_ref, gk_ref, o_ref, kn, vt, kmean, chosen, s_scr, s_own = refs[3 * hps:]
    qi = pl.program_id(2)
    blk = MOBA_BLOCK
    grp = MOBA_GROUP
    heads = range(hps)

    @pl.when(qi == 0)
    def _():
        for hh in heads:
            for n in range(n_blk):
                kb = _rms_rows(k_refs[hh][n * blk:(n + 1) * blk, :].astype(F32), gk_ref[...])
                kn[hh, n * blk:(n + 1) * blk, :] = kb.astype(BF16)
                kmean[hh, n:n + 1, :] = jnp.mean(kb, axis=0, keepdims=True)
                vt[hh, n] = v_refs[hh][n * blk:(n + 1) * blk, :].astype(F32).T.astype(BF16)

    brow = lax.broadcasted_iota(jnp.int32, (n_blk, blk), 0)
    past = brow < qi
    k_i = lax.broadcasted_iota(jnp.int32, (blk, blk), 0)
    q_i = lax.broadcasted_iota(jnp.int32, (blk, blk), 1)
    own = pl.multiple_of(qi * blk, blk)
    q_bf, m_run0 = [], []
    for hh in heads:
        qn = _rms_rows(q_refs[hh][...].astype(F32), gq_ref[...])
        q_bf.append((qn * (MOBA_HD ** -0.5 * LOG2_E)).astype(BF16))
        sb = lax.dot_general(kmean[hh], qn, NT_DIMS, precision=lax.Precision.HIGHEST,
                             preferred_element_type=F32)
        sb = jnp.where(past, sb, NEG_INF)
        rank = jnp.zeros((n_blk, blk), F32)
        for m in range(n_blk):
            cm = sb[m:m + 1, :]
            ge = jnp.where(cm >= sb, 1.0, 0.0)
            gt = jnp.where(cm > sb, 1.0, 0.0)
            rank = rank + jnp.where(brow > m, ge, gt)
        chosen[hh] = jnp.where(past, jnp.where(rank < float(MOBA_TOPK), 1.0, 0.0), 0.0)
        so = lax.dot_general(kn[hh, pl.ds(own, blk), :], q_bf[hh], NT_DIMS, preferred_element_type=F32)
        so = jnp.where(k_i <= q_i, so, NEG_INF)
        s_own[hh] = so
        m_run0.append(_fold_rows(so, jnp.max))
    n_grp = (qi + grp - 1) // grp

    def scores(g, m_runs):
        start = pl.multiple_of(g * (grp * blk), grp * blk)
        out = []
        for hh in heads:
            m_run = m_runs[hh]
            sg = lax.dot_general(kn[hh, pl.ds(start, grp * blk), :], q_bf[hh], NT_DIMS,
                                 preferred_element_type=F32)
            for b in range(grp):
                n = g * grp + b
                sn = jnp.where(chosen[hh, pl.ds(n, 1), :] > 0.0, sg[b * blk:(b + 1) * blk, :], NEG_INF)
                s_scr[hh, n] = sn
                m_run = jnp.maximum(m_run, _fold_rows(sn, jnp.max))
            out.append(m_run)
        return tuple(out)

    m_runs = lax.fori_loop(0, n_grp, scores, tuple(m_run0))
    m_rows = [jnp.max(m_runs[hh], axis=0, keepdims=True) for hh in heads]

    carry0 = []
    for hh in heads:
        p_own = jnp.exp2(s_own[hh] - m_rows[hh])
        carry0.append(_fold_rows(p_own, jnp.sum))
        carry0.append(jnp.dot(vt[hh, qi], p_own.astype(BF16), preferred_element_type=F32))

    def values(g, carry):
        out = []
        for hh in heads:
            l_run, acc = carry[2 * hh], carry[2 * hh + 1]
            ps, vs = [], []
            for b in range(grp):
                n = g * grp + b
                pn = jnp.exp2(s_scr[hh, n] - m_rows[hh])
                l_run = l_run + _fold_rows(pn, jnp.sum)
                ps.append(pn.astype(BF16))
                vs.append(vt[hh, n])
            acc = acc + jnp.dot(jnp.concatenate(vs, axis=1), jnp.concatenate(ps, axis=0),
                                preferred_element_type=F32)
            out += [l_run, acc]
        return tuple(out)

    carry = lax.fori_loop(0, n_grp, values, tuple(carry0))
    for hh in heads:
        l_row = jnp.sum(carry[2 * hh], axis=0, keepdims=True)
        o_ref[:, hh * MOBA_HD:(hh + 1) * MOBA_HD] = (carry[2 * hh + 1] / l_row).T.astype(o_ref.dtype)


def _moba(z3, gq, gk):
    b_sz, s, _ = z3.shape
    n_blk = s // MOBA_BLOCK
    hd = MOBA_HD
    hps = MOBA_HPS

    def head_spec(rows, off, hh, row_index):
        return pl.BlockSpec((None, rows, hd), lambda b, h, i: (b, row_index(i), off // hd + h * hps + hh))

    q_specs = [head_spec(MOBA_BLOCK, OFF_MQ, hh, lambda i: i) for hh in range(hps)]
    k_specs = [head_spec(s, OFF_MK, hh, lambda i: 0) for hh in range(hps)]
    v_specs = [head_spec(s, OFF_MV, hh, lambda i: 0) for hh in range(hps)]
    vec = pl.BlockSpec((1, hd), lambda b, h, i: (0, 0))
    return pl.pallas_call(
        functools.partial(_moba_kernel, n_blk=n_blk),
        out_shape=jax.ShapeDtypeStruct((b_sz, s, MOBA_HEADS * hd), BF16),
        grid=(b_sz, MOBA_HEADS // hps, n_blk),
        in_specs=q_specs + k_specs + v_specs + [vec, vec],
        out_specs=pl.BlockSpec((None, MOBA_BLOCK, hps * hd), lambda b, h, i: (b, i, h)),
        scratch_shapes=[
            pltpu.VMEM((hps, s, hd), BF16),
            pltpu.VMEM((hps, n_blk, hd, MOBA_BLOCK), BF16),
            pltpu.VMEM((hps, n_blk, hd), F32),
            pltpu.VMEM((hps, n_blk, MOBA_BLOCK), F32),
            pltpu.VMEM((hps, n_blk, MOBA_BLOCK, MOBA_BLOCK), F32),
            pltpu.VMEM((hps, MOBA_BLOCK, MOBA_BLOCK), F32),
        ],
        compiler_params=_cparams(("arbitrary", "arbitrary", "arbitrary")),
        name="moba",
    )(*([z3] * (3 * hps)), gq, gk)


MERGE_COLS = 512


def _merge_kernel(a_ref, gl_ref, mo_ref, gt_ref, x_ref, wc_ref, wg_ref, wm_ref, wo_ref, gn_ref,
                  wup_ref, wdn_ref, o_ref, hn_ref, wup_o, wdn_o, mg):
    d = D_MODEL
    wup_o[...] = wup_ref[...].astype(wup_o.dtype)
    wdn_o[...] = wdn_ref[...].astype(wdn_o.dtype)
    for c0 in range(0, d, MERGE_COLS):
        cs = slice(c0, c0 + MERGE_COLS)
        ya = jnp.dot(a_ref[...], wc_ref[:, cs], preferred_element_type=F32)
        yb = jnp.dot(gl_ref[...], wg_ref[:, cs], preferred_element_type=F32)
        yc = jnp.dot(mo_ref[...], wm_ref[:, cs], preferred_element_type=F32)
        g0 = gt_ref[:, c0:c0 + MERGE_COLS].astype(F32)
        g1 = gt_ref[:, d + c0:d + c0 + MERGE_COLS].astype(F32)
        g2 = gt_ref[:, 2 * d + c0:2 * d + c0 + MERGE_COLS].astype(F32)
        mg[:, cs] = (g0 * ya + g1 * yb + g2 * yc).astype(BF16)
    x_new = x_ref[...] + jnp.dot(mg[...], wo_ref[...], preferred_element_type=F32)
    o_ref[...] = x_new
    hn_ref[...] = _rms_rows(x_new, gn_ref[...]).astype(hn_ref.dtype)


def _resident(w_stack, layer):
    return pl.BlockSpec((None,) + w_stack.shape[1:], lambda i: (layer, 0, 0), pipeline_mode=pl.Buffered(1))


def _merge(a_act, o_gla, o_moba, gates, x, wc, wg, wm, wo, g_next, w_up, w_down, layer, *, tm=256):
    m, d = x.shape
    steps = m // tm
    up_rows, dn_rows = w_up.shape[1] // steps, w_down.shape[1] // steps

    def rows(width):
        return pl.BlockSpec((tm, width), lambda i: (i, 0))

    return pl.pallas_call(
        _merge_kernel,
        out_shape=(jax.ShapeDtypeStruct((m, d), F32), jax.ShapeDtypeStruct((m, d), BF16),
                   jax.ShapeDtypeStruct(w_up.shape[1:], BF16), jax.ShapeDtypeStruct(w_down.shape[1:], BF16)),
        grid=(steps,),
        in_specs=[rows(a_act.shape[1]), rows(o_gla.shape[1]), rows(o_moba.shape[1]), rows(gates.shape[1]),
                  rows(d), _resident(wc, layer), _resident(wg, layer), _resident(wm, layer),
                  _resident(wo, layer), _layer_vec(d, layer),
                  pl.BlockSpec((None, up_rows, w_up.shape[2]), lambda i: (layer, i, 0)),
                  pl.BlockSpec((None, dn_rows, w_down.shape[2]), lambda i: (layer, i, 0))],
        out_specs=(rows(d), rows(d),
                   pl.BlockSpec((up_rows, w_up.shape[2]), lambda i: (i, 0)),
                   pl.BlockSpec((dn_rows, w_down.shape[2]), lambda i: (i, 0))),
        scratch_shapes=[pltpu.VMEM((tm, d), BF16)],
        compiler_params=_cparams(("arbitrary",)),
        name="merge",
    )(a_act, o_gla, o_moba, gates, x, wc, wg, wm, wo, g_next, w_up, w_down)


def _shift_rows(u, k, carry):
    rolled = pltpu.roll(u, k, 0)
    head = jnp.where(lax.broadcasted_iota(jnp.int32, carry.shape, 0) < k,
                     pltpu.roll(carry, k, 0), rolled[:SUBLANE, :])
    return jnp.concatenate([head, rolled[SUBLANE:, :]], axis=0)


def _ffn_kernel(x_ref, h_ref, wug_ref, wuv_ref, cwg_ref, cwv_ref, wd_ref, gn_ref, o_ref, hn_ref, cg, cv,
                *, tiles_per_seq):
    i = pl.program_id(0)
    j = pl.program_id(1)
    tm = x_ref.shape[0]

    @pl.when(j == 0)
    def _():
        o_ref[...] = x_ref[...]

    @pl.when(i % tiles_per_seq == 0)
    def _():
        cg[j] = jnp.zeros(cg.shape[1:], F32)
        cv[j] = jnp.zeros(cv.shape[1:], F32)

    h = h_ref[...]

    def conv(w_ref, cw_ref, carry_ref):
        u = jnp.dot(h, w_ref[...], preferred_element_type=F32)
        carry = carry_ref[j]
        out = (u * cw_ref[2:3, :] + _shift_rows(u, 1, carry) * cw_ref[1:2, :]
               + _shift_rows(u, 2, carry) * cw_ref[0:1, :])
        carry_ref[j] = u[tm - SUBLANE:, :]
        return out

    ug = conv(wug_ref, cwg_ref, cg)
    uv = conv(wuv_ref, cwv_ref, cv)
    act = (_silu(ug) * uv).astype(BF16)
    o_ref[...] += jnp.dot(act, wd_ref[...], preferred_element_type=F32)

    @pl.when(j == pl.num_programs(1) - 1)
    def _():
        hn_ref[...] = _rms_rows(o_ref[...], gn_ref[...]).astype(hn_ref.dtype)


def _ffn(x, h, w_up, conv_w, w_down, g_next, layer, *, seq, tm=512, tf=512):
    m, d = x.shape
    f = w_down.shape[0]
    nf = f // tf
    return pl.pallas_call(
        functools.partial(_ffn_kernel, tiles_per_seq=seq // tm),
        out_shape=(jax.ShapeDtypeStruct((m, d), F32), jax.ShapeDtypeStruct((m, d), BF16)),
        grid=(m // tm, nf),
        in_specs=[
            pl.BlockSpec((tm, d), lambda i, j: (i, 0)),
            pl.BlockSpec((tm, d), lambda i, j: (i, 0)),
            pl.BlockSpec((d, tf), lambda i, j: (0, j)),
            pl.BlockSpec((d, tf), lambda i, j: (0, j + nf)),
            pl.BlockSpec((None, FFN_CONV_WIDTH, tf), lambda i, j: (layer, 0, j)),
            pl.BlockSpec((None, FFN_CONV_WIDTH, tf), lambda i, j: (layer, 0, j + nf)),
            pl.BlockSpec((tf, d), lambda i, j: (j, 0)),
            pl.BlockSpec((None, 1, d), lambda i, j: (layer, 0, 0)),
        ],
        out_specs=(pl.BlockSpec((tm, d), lambda i, j: (i, 0)), pl.BlockSpec((tm, d), lambda i, j: (i, 0))),
        scratch_shapes=[
            pltpu.VMEM((nf, SUBLANE, tf), F32),
            pltpu.VMEM((nf, SUBLANE, tf), F32),
        ],
        compiler_params=_cparams(("arbitrary", "arbitrary")),
        name="conv_ffn",
    )(x, h, w_up, w_up, conv_w, conv_w, w_down, g_next)


PLE_COLS = 512


def _ple_kernel(x_ref, h_ref, p_ref, wg_ref, wp_ref, *rest):
    o_ref = rest[-2] if len(rest) == 3 else rest[-1]
    h = h_ref[...]
    p_bf = p_ref[...].astype(BF16)
    for c0 in range(0, D_MODEL, PLE_COLS):
        cs = slice(c0, c0 + PLE_COLS)
        gate = _sigmoid(jnp.dot(h, wg_ref[:, cs], preferred_element_type=F32))
        emb = jnp.dot(p_bf, wp_ref[:, cs], preferred_element_type=F32)
        o_ref[:, cs] = x_ref[:, cs] + gate * emb
    if len(rest) == 3:
        gn_ref, _, hn_ref = rest
        hn_ref[...] = _rms_rows(o_ref[...], gn_ref[...]).astype(hn_ref.dtype)


def _ple(x, h, p, w_gate, w_ple, layer, g_next, *, tm=512):
    m, d = x.shape
    rows = lambda width: pl.BlockSpec((tm, width), lambda i: (i, 0))
    in_specs = [rows(d), rows(d), pl.BlockSpec((None, tm, PLE_DIM), lambda i: (layer, i, 0)),
                _resident(w_gate, layer), _resident(w_ple, layer)]
    args = [x, h, p, w_gate, w_ple]
    out_shape = [jax.ShapeDtypeStruct((m, d), F32)]
    out_specs = [rows(d)]
    if g_next is not None:
        in_specs.append(_layer_vec(d, layer + 1))
        args.append(g_next)
        out_shape.append(jax.ShapeDtypeStruct((m, d), BF16))
        out_specs.append(rows(d))
    return pl.pallas_call(
        _ple_kernel,
        out_shape=tuple(out_shape),
        grid=(m // tm,),
        in_specs=in_specs,
        out_specs=tuple(out_specs),
        compiler_params=_cparams(("arbitrary",)),
        name="ple",
    )(*args)


def _pad_in_proj_kernel(w_ref, o_ref):
    ga_end = OFF_GA + GLA_GATE_RANK
    o_ref[:, :ga_end] = w_ref[:, :ga_end].astype(o_ref.dtype)
    o_ref[:, ga_end:OFF_MQ] = jnp.zeros((o_ref.shape[0], OFF_MQ - ga_end), o_ref.dtype)
    o_ref[:, OFF_MQ:] = w_ref[:, ga_end:].astype(o_ref.dtype)


def _pad_in_proj(w_in, *, tr=256):
    depth, d, n = w_in.shape
    return pl.pallas_call(
        _pad_in_proj_kernel,
        out_shape=jax.ShapeDtypeStruct((depth, d, Z_COLS), BF16),
        grid=(depth, d // tr),
        in_specs=[pl.BlockSpec((None, tr, n), lambda l, r: (l, r, 0))],
        out_specs=pl.BlockSpec((None, tr, Z_COLS), lambda l, r: (l, r, 0)),
        compiler_params=_cparams(("arbitrary", "arbitrary")),
        name="pad_in_proj",
    )(w_in)


def kernel(x, p, norm_mix_g, w_in, conv_w, conv_b, conv_ln_g, conv_ln_b, w_conv_out, gla_w_a2, gla_b_a,
           gla_norm_g, w_gla_out, moba_q_norm_g, moba_k_norm_g, w_moba_out, w_gate, b_gate, w_out,
           norm_ffn_g, w_up, ffn_conv_w, w_down, norm_ple_g, w_ple_gate, w_ple):
    b_sz, s, d = x.shape
    depth = w_in.shape[0]
    m = b_sz * s
    xf = x.reshape(m, d)
    p2 = p.reshape(depth, m, PLE_DIM)
    row = lambda v: v.reshape(1, -1)
    stack_vec = lambda v: v.reshape(depth, 1, -1)
    w_in_p = _pad_in_proj(w_in)
    wc_b, wg_b, wm_b, wo_b = (w.astype(BF16) for w in (w_conv_out, w_gla_out, w_moba_out, w_out))
    w_pg_b, w_ple_b = w_ple_gate.astype(BF16), w_ple.astype(BF16)
    w_a2p = jnp.concatenate(
        [gla_w_a2, jnp.zeros((depth, GATE_PAD - GLA_GATE_RANK, GLA_QK), F32)], axis=1).astype(BF16)
    g_mix, g_ffn, g_ple, b_gate3 = (stack_vec(v) for v in (norm_mix_g, norm_ffn_g, norm_ple_g, b_gate))
    conv_b3, conv_ln_g3, conv_ln_b3 = (stack_vec(v) for v in (conv_b, conv_ln_g, conv_ln_b))

    h = _rmsnorm(xf, g_mix, 0)
    for i in range(depth):
        z = _proj(h, w_in_p, None, i, tm=2048, tn=1152, name="in_proj")
        gates = _proj(h, w_gate, b_gate3, i, tm=2048, tn=768, name="gate_proj")
        z3 = z.reshape(b_sz, s, Z_COLS)
        a_act = _conformer_conv(z3, conv_w, conv_b3, conv_ln_g3, conv_ln_b3, i)
        o_gla = _gla(z3, w_a2p[i], row(gla_b_a[i]), row(gla_norm_g[i]))
        o_moba = _moba(z3, row(moba_q_norm_g[i]), row(moba_k_norm_g[i]))
        xf, h, w_up_b, w_down_b = _merge(a_act.reshape(m, -1), o_gla.reshape(m, -1), o_moba.reshape(m, -1),
                                         gates, xf, wc_b, wg_b, wm_b, wo_b, g_ffn, w_up, w_down, i)
        xf, h = _ffn(xf, h, w_up_b, ffn_conv_w, w_down_b, g_ple, i, seq=s)
        if i + 1 < depth:
            xf, h = _ple(xf, h, p2, w_pg_b, w_ple_b, i, g_mix)
        else:
            (xf,) = _ple(xf, h, p2, w_pg_b, w_ple_b, i, None)
    return xf.reshape(b_sz, s, d)
```

```python
import functools

import jax
import jax.numpy as jnp
from jax import lax
from jax.experimental import pallas as pl
from jax.experimental.pallas import tpu as pltpu

F32 = jnp.float32
BF16 = jnp.bfloat16

D_MODEL = 2048
PLE_DIM = 256
EPS = 1e-6
CONV_CH = 512
CONV_WIDTH = 31
GLA_HEADS = 4
GLA_DV = 128
GLA_DK = 64
GLA_GATE_RANK = 16
GLA_TAU = 16.0
GLA_CHUNK = 64
GLA_SUB = 16
MOBA_HEADS = 8
MOBA_HD = 128
MOBA_BLOCK = 256
MOBA_TOPK = 3
FFN_DIM = 5632
FFN_CONV_WIDTH = 3

LANE = 128
SUBLANE = 8
GATE_PAD = LANE
OFF_GLU = 0
OFF_GQ = OFF_GLU + 2 * CONV_CH
OFF_GK = OFF_GQ + GLA_HEADS * GLA_DK
OFF_GV = OFF_GK + GLA_HEADS * GLA_DK
OFF_GR = OFF_GV + GLA_HEADS * GLA_DV
OFF_GA = OFF_GR + GLA_HEADS * GLA_DV
OFF_MQ = OFF_GA + GATE_PAD
OFF_MK = OFF_MQ + MOBA_HEADS * MOBA_HD
OFF_MV = OFF_MK + MOBA_HEADS * MOBA_HD
Z_COLS = OFF_MV + MOBA_HEADS * MOBA_HD

VMEM_LIMIT = 56 * 1024 * 1024


def _cparams(sem):
    return pltpu.CompilerParams(dimension_semantics=sem, vmem_limit_bytes=VMEM_LIMIT)


def _sigmoid(x):
    return 1.0 / (1.0 + jnp.exp(-x))


def _silu(x):
    return x * _sigmoid(x)


def _rms_rows(x, g):
    ms = jnp.mean(x * x, axis=-1, keepdims=True)
    return x * lax.rsqrt(ms + EPS) * g


def _layer_vec(width, layer):
    return pl.BlockSpec((None, 1, width), lambda i: (layer, 0, 0))


def _rmsnorm_kernel(x_ref, g_ref, o_ref):
    o_ref[...] = _rms_rows(x_ref[...], g_ref[...]).astype(o_ref.dtype)


def _rmsnorm(x, g_stack, layer, *, tm=1024):
    m, d = x.shape
    return pl.pallas_call(
        _rmsnorm_kernel,
        out_shape=jax.ShapeDtypeStruct((m, d), BF16),
        grid=(m // tm,),
        in_specs=[pl.BlockSpec((tm, d), lambda i: (i, 0)), _layer_vec(d, layer)],
        out_specs=pl.BlockSpec((tm, d), lambda i: (i, 0)),
        compiler_params=_cparams(("arbitrary",)),
        name="rmsnorm",
    )(x, g_stack)


def _proj_kernel(h_ref, w_ref, o_ref):
    o_ref[...] = jnp.dot(h_ref[...], w_ref[...], preferred_element_type=F32).astype(o_ref.dtype)


def _proj_sigmoid_kernel(h_ref, w_ref, b_ref, o_ref):
    acc = jnp.dot(h_ref[...], w_ref[...].astype(BF16), preferred_element_type=F32)
    o_ref[...] = _sigmoid(acc + b_ref[...]).astype(o_ref.dtype)


def _proj(h, w_stack, b_stack, layer, *, tm, tn, name):
    m, d = h.shape
    n = w_stack.shape[2]
    in_specs = [pl.BlockSpec((tm, d), lambda i, j: (i, 0)),
                pl.BlockSpec((None, d, tn), lambda i, j: (layer, 0, j))]
    args = [h, w_stack]
    if b_stack is not None:
        in_specs.append(pl.BlockSpec((None, 1, tn), lambda i, j: (layer, 0, j)))
        args.append(b_stack)
    return pl.pallas_call(
        _proj_kernel if b_stack is None else _proj_sigmoid_kernel,
        out_shape=jax.ShapeDtypeStruct((m, n), BF16),
        grid=(m // tm, n // tn),
        in_specs=in_specs,
        out_specs=pl.BlockSpec((tm, tn), lambda i, j: (i, j)),
        compiler_params=_cparams(("arbitrary", "arbitrary")),
        name=name,
    )(*args)


CONV_HALO = 32
CONV_ROWS = 64


def _conv_kernel(z_ref, w_ref, cb_ref, lg_ref, lb_ref, o_ref, abuf, sh, *, ts):
    t = pl.program_id(1)

    @pl.when(t == 0)
    def _():
        abuf[0:CONV_HALO, :] = jnp.zeros((CONV_HALO, CONV_CH), F32)

    @pl.when(t > 0)
    def _():
        abuf[0:CONV_HALO, :] = abuf[ts:ts + CONV_HALO, :]

    z = z_ref[...].astype(F32)
    abuf[CONV_HALO:CONV_HALO + ts, :] = z[:, :CONV_CH] * _sigmoid(z[:, CONV_CH:])

    sh_rows = sh.shape[1]
    for b in range(1, SUBLANE):
        sh[b] = abuf[b:b + sh_rows, :]

    first = CONV_HALO - (CONV_WIDTH - 1)
    for r0 in range(0, ts, CONV_ROWS):
        acc = abuf[r0 + CONV_HALO:r0 + CONV_HALO + CONV_ROWS, :] * w_ref[CONV_WIDTH - 1:CONV_WIDTH, :]
        for j in range(CONV_WIDTH - 1):
            res = (first + j) % SUBLANE
            base = r0 + first + j - res
            rows = abuf[base:base + CONV_ROWS, :] if res == 0 else sh[res, base:base + CONV_ROWS, :]
            acc = acc + rows * w_ref[j:j + 1, :]
        acc = acc + cb_ref[...]
        mu = jnp.mean(acc, axis=-1, keepdims=True)
        xc = acc - mu
        var = jnp.mean(xc * xc, axis=-1, keepdims=True)
        y = xc * lax.rsqrt(var + EPS) * lg_ref[...] + lb_ref[...]
        o_ref[r0:r0 + CONV_ROWS, :] = _silu(y).astype(o_ref.dtype)


def _conformer_conv(z3, conv_w, conv_b, ln_g, ln_b, layer, *, ts=512):
    b_sz, s, _ = z3.shape

    def par(a):
        return pl.BlockSpec((None,) + a.shape[1:], lambda b, t: (layer, 0, 0))

    return pl.pallas_call(
        functools.partial(_conv_kernel, ts=ts),
        out_shape=jax.ShapeDtypeStruct((b_sz, s, CONV_CH), BF16),
        grid=(b_sz, s // ts),
        in_specs=[
            pl.BlockSpec((None, ts, 2 * CONV_CH), lambda b, t: (b, t, OFF_GLU // (2 * CONV_CH))),
            par(conv_w), par(conv_b), par(ln_g), par(ln_b),
        ],
        out_specs=pl.BlockSpec((None, ts, CONV_CH), lambda b, t: (b, t, 0)),
        scratch_shapes=[pltpu.VMEM((CONV_HALO + ts, CONV_CH), F32),
                        pltpu.VMEM((SUBLANE, CONV_HALO + ts - SUBLANE, CONV_CH), F32)],
        compiler_params=_cparams(("arbitrary", "arbitrary")),
        name="conformer_conv",
    )(z3, conv_w, conv_b, ln_g, ln_b)


LOG2_E = 1.4426950408889634
GLA_QK = GLA_HEADS * GLA_DK
GLA_V = GLA_HEADS * GLA_DV


GLA_CPS = 4


def _gla_kernel(q_ref, k_ref, v_ref, r_ref, al_ref, wa_ref, ba_ref, ng_ref, o_ref,
                st, qf, kf, vf, bcs, pbig, intra):
    @pl.when(pl.program_id(1) == 0)
    def _():
        st[...] = jnp.zeros(st.shape, F32)

    for ci in range(GLA_CPS):
        rs = slice(ci * GLA_CHUNK, (ci + 1) * GLA_CHUNK)
        _gla_chunk(q_ref.at[rs, :], k_ref.at[rs, :], v_ref.at[rs, :], r_ref.at[rs, :], al_ref.at[rs, :],
                   wa_ref, ba_ref, ng_ref, o_ref.at[rs, :],
                   st, qf.at[ci], kf.at[ci], vf.at[ci], bcs.at[ci], pbig.at[ci], intra.at[ci])


def _gla_chunk(q_ref, k_ref, v_ref, r_ref, al_ref, wa_ref, ba_ref, ng_ref, o_ref,
               st, qf, kf, vf, bcs, pbig, intra):
    c = GLA_CHUNK

    xa = jnp.dot(al_ref[...], wa_ref[...], preferred_element_type=F32) + ba_ref[...]
    la = (jnp.minimum(xa, 0.0) - jnp.log(1.0 + jnp.exp(-jnp.abs(xa)))) * (1.0 / GLA_TAU)
    row = lax.broadcasted_iota(jnp.int32, (c, c), 0)
    col = lax.broadcasted_iota(jnp.int32, (c, c), 1)
    tri = jnp.where(row >= col, 1.0, 0.0).astype(BF16)
    la_hi = la.astype(BF16)
    la_lo = (la - la_hi.astype(F32)).astype(BF16)
    bc = (jnp.dot(tri, la_hi, preferred_element_type=F32)
          + jnp.dot(tri, la_lo, preferred_element_type=F32))

    q = q_ref[...].astype(F32) * (GLA_DK ** -0.5)
    k = k_ref[...].astype(F32)
    qf[...] = q
    kf[...] = k
    vf[...] = v_ref[...].astype(F32)
    bcs[...] = bc * LOG2_E

    b_last = bc[c - 1:c, :]
    q_in = (q * jnp.exp(bc)).astype(BF16)
    k_dec = (k * jnp.exp(b_last - bc)).astype(BF16)
    e_last = jnp.exp(b_last)
    v_bf = v_ref[...]

    srow = lax.broadcasted_iota(jnp.int32, (GLA_QK, GLA_V), 0) // GLA_DK
    scol = lax.broadcasted_iota(jnp.int32, (GLA_QK, GLA_V), 1) // GLA_DV
    sel = jnp.where(srow == scol, 1.0, 0.0).astype(BF16)

    for j in range(c // GLA_SUB):
        t0 = j * GLA_SUB
        tj = c - t0
        q_rows = qf[t0:c, :]
        bc_rows = bcs[t0:c, :]
        t_loc = lax.broadcasted_iota(jnp.int32, (GLA_SUB, GLA_QK), 0)
        for sl in range(GLA_SUB):
            s = t0 + sl
            d = bc_rows - bcs[s:s + 1, :]
            d_own = jnp.minimum(d[:GLA_SUB, :], 0.0)
            d = d_own if tj == GLA_SUB else jnp.concatenate([d_own, d[GLA_SUB:, :]], axis=0)
            p = (q_rows * kf[s:s + 1, :]) * jnp.exp2(d)
            p_own = jnp.where(t_loc >= sl, p[:GLA_SUB, :], 0.0)
            p = p_own if tj == GLA_SUB else jnp.concatenate([p_own, p[GLA_SUB:, :]], axis=0)
            pbig[sl * tj:(sl + 1) * tj, :] = p.astype(BF16)
        r_all = jnp.dot(pbig[0:GLA_SUB * tj, :], sel, preferred_element_type=F32)
        acc = jnp.zeros((tj, GLA_V), F32)
        for sl in range(GLA_SUB):
            s = t0 + sl
            acc = acc + r_all[sl * tj:(sl + 1) * tj, :] * vf[s:s + 1, :]
        if j == 0:
            intra[...] = acc
        else:
            intra[t0:c, :] = intra[t0:c, :] + acc

    for h in range(GLA_HEADS):
        ks = slice(h * GLA_DK, (h + 1) * GLA_DK)
        vs = slice(h * GLA_DV, (h + 1) * GLA_DV)
        st_h = st[h]
        inter = lax.dot_general(q_in[:, ks], st_h.astype(BF16), (((1,), (1,)), ((), ())),
                                preferred_element_type=F32)
        upd = lax.dot_general(v_bf[:, vs], k_dec[:, ks], (((0,), (0,)), ((), ())),
                              preferred_element_type=F32)
        st[h] = st_h * e_last[:, ks] + upd
        o_h = inter + intra[:, vs]
        y = _rms_rows(o_h, ng_ref[...])
        o_ref[:, vs] = (y * _silu(r_ref[:, vs].astype(F32))).astype(o_ref.dtype)


def _gla(z3, w_a2p, b_a, norm_g):
    b_sz, s, _ = z3.shape
    c = GLA_CHUNK
    rows = c * GLA_CPS
    cps = GLA_CPS

    def zspec(width, off):
        return pl.BlockSpec((None, rows, width), lambda b, t: (b, t, off // width))

    return pl.pallas_call(
        _gla_kernel,
        out_shape=jax.ShapeDtypeStruct((b_sz, s, GLA_V), BF16),
        grid=(b_sz, s // rows),
        in_specs=[
            zspec(GLA_QK, OFF_GQ), zspec(GLA_QK, OFF_GK), zspec(GLA_V, OFF_GV), zspec(GLA_V, OFF_GR),
            zspec(GATE_PAD, OFF_GA),
            pl.BlockSpec((GATE_PAD, GLA_QK), lambda b, t: (0, 0)),
            pl.BlockSpec((1, GLA_QK), lambda b, t: (0, 0)),
            pl.BlockSpec((1, GLA_DV), lambda b, t: (0, 0)),
        ],
        out_specs=pl.BlockSpec((None, rows, GLA_V), lambda b, t: (b, t, 0)),
        scratch_shapes=[
            pltpu.VMEM((GLA_HEADS, GLA_DV, GLA_DK), F32),
            pltpu.VMEM((cps, c, GLA_QK), F32),
            pltpu.VMEM((cps, c, GLA_QK), F32),
            pltpu.VMEM((cps, c, GLA_V), F32),
            pltpu.VMEM((cps, c, GLA_QK), F32),
            pltpu.VMEM((cps, GLA_SUB * c, GLA_QK), BF16),
            pltpu.VMEM((cps, c, GLA_V), F32),
        ],
        compiler_params=_cparams(("arbitrary", "arbitrary")),
        name="gla",
    )(z3, z3, z3, z3, z3, w_a2p, b_a, norm_g)


NEG_INF = float("-inf")
MOBA_GROUP = 4
NT_DIMS = (((1,), (1,)), ((), ()))


def _fold_rows(x, op):
    return op(x.reshape(x.shape[0] // SUBLANE, SUBLANE, x.shape[1]), axis=0)


MOBA_HPS = 4


def _moba_kernel(*refs, n_blk):
    hps = MOBA_HPS
    q_refs, k_refs, v_refs = refs[:hps], refs[hps:2 * hps], refs[2 * hps:3 * hps]
    gq_ref, gk_ref, o_ref, kn, vt, kmean, chosen, s_scr, s_own = refs[3 * hps:]
    qi = pl.program_id(2)
    blk = MOBA_BLOCK
    grp = MOBA_GROUP
    heads = range(hps)

    @pl.when(qi == 0)
    def _():
        for hh in heads:
            for n in range(n_blk):
                kb = _rms_rows(k_refs[hh][n * blk:(n + 1) * blk, :].astype(F32), gk_ref[...])
                kn[hh, n * blk:(n + 1) * blk, :] = kb.astype(BF16)
                kmean[hh, n:n + 1, :] = jnp.mean(kb, axis=0, keepdims=True)
                vt[hh, n] = v_refs[hh][n * blk:(n + 1) * blk, :].astype(F32).T.astype(BF16)

    brow = lax.broadcasted_iota(jnp.int32, (n_blk, blk), 0)
    past = brow < qi
    k_i = lax.broadcasted_iota(jnp.int32, (blk, blk), 0)
    q_i = lax.broadcasted_iota(jnp.int32, (blk, blk), 1)
    own = pl.multiple_of(qi * blk, blk)
    q_bf, m_run0 = [], []
    for hh in heads:
        qn = _rms_rows(q_refs[hh][...].astype(F32), gq_ref[...])
        q_bf.append((qn * (MOBA_HD ** -0.5 * LOG2_E)).astype(BF16))
        sb = lax.dot_general(kmean[hh], qn, NT_DIMS, precision=lax.Precision.HIGHEST,
                             preferred_element_type=F32)
        sb = jnp.where(past, sb, NEG_INF)
        rank = jnp.zeros((n_blk, blk), F32)
        for m in range(n_blk):
            cm = sb[m:m + 1, :]
            ge = jnp.where(cm >= sb, 1.0, 0.0)
            gt = jnp.where(cm > sb, 1.0, 0.0)
            rank = rank + jnp.where(brow > m, ge, gt)
        chosen[hh] = jnp.where(past, jnp.where(rank < float(MOBA_TOPK), 1.0, 0.0), 0.0)
        so = lax.dot_general(kn[hh, pl.ds(own, blk), :], q_bf[hh], NT_DIMS, preferred_element_type=F32)
        so = jnp.where(k_i <= q_i, so, NEG_INF)
        s_own[hh] = so
        m_run0.append(_fold_rows(so, jnp.max))
    n_grp = (qi + grp - 1) // grp

    def scores(g, m_runs):
        start = pl.multiple_of(g * (grp * blk), grp * blk)
        out = []
        for hh in heads:
            m_run = m_runs[hh]
            sg = lax.dot_general(kn[hh, pl.ds(start, grp * blk), :], q_bf[hh], NT_DIMS,
                                 preferred_element_type=F32)
            for b in range(grp):
                n = g * grp + b
                sn = jnp.where(chosen[hh, pl.ds(n, 1), :] > 0.0, sg[b * blk:(b + 1) * blk, :], NEG_INF)
                s_scr[hh, n] = sn
                m_run = jnp.maximum(m_run, _fold_rows(sn, jnp.max))
            out.append(m_run)
        return tuple(out)

    m_runs = lax.fori_loop(0, n_grp, scores, tuple(m_run0))
    m_rows = [jnp.max(m_runs[hh], axis=0, keepdims=True) for hh in heads]

    carry0 = []
    for hh in heads:
        p_own = jnp.exp2(s_own[hh] - m_rows[hh])
        carry0.append(_fold_rows(p_own, jnp.sum))
        carry0.append(jnp.dot(vt[hh, qi], p_own.astype(BF16), preferred_element_type=F32))

    def values(g, carry):
        out = []
        for hh in heads:
            l_run, acc = carry[2 * hh], carry[2 * hh + 1]
            ps, vs = [], []
            for b in range(grp):
                n = g * grp + b
                pn = jnp.exp2(s_scr[hh, n] - m_rows[hh])
                l_run = l_run + _fold_rows(pn, jnp.sum)
                ps.append(pn.astype(BF16))
                vs.append(vt[hh, n])
            acc = acc + jnp.dot(jnp.concatenate(vs, axis=1), jnp.concatenate(ps, axis=0),
                                preferred_element_type=F32)
            out += [l_run, acc]
        return tuple(out)

    carry = lax.fori_loop(0, n_grp, values, tuple(carry0))
    for hh in heads:
        l_row = jnp.sum(carry[2 * hh], axis=0, keepdims=True)
        o_ref[:, hh * MOBA_HD:(hh + 1) * MOBA_HD] = (carry[2 * hh + 1] / l_row).T.astype(o_ref.dtype)


def _moba(z3, gq, gk):
    b_sz, s, _ = z3.shape
    n_blk = s // MOBA_BLOCK
    hd = MOBA_HD
    hps = MOBA_HPS

    def head_spec(rows, off, hh, row_index):
        return pl.BlockSpec((None, rows, hd), lambda b, h, i: (b, row_index(i), off // hd + h * hps + hh))

    q_specs = [head_spec(MOBA_BLOCK, OFF_MQ, hh, lambda i: i) for hh in range(hps)]
    k_specs = [head_spec(s, OFF_MK, hh, lambda i: 0) for hh in range(hps)]
    v_specs = [head_spec(s, OFF_MV, hh, lambda i: 0) for hh in range(hps)]
    vec = pl.BlockSpec((1, hd), lambda b, h, i: (0, 0))
    return pl.pallas_call(
        functools.partial(_moba_kernel, n_blk=n_blk),
        out_shape=jax.ShapeDtypeStruct((b_sz, s, MOBA_HEADS * hd), BF16),
        grid=(b_sz, MOBA_HEADS // hps, n_blk),
        in_specs=q_specs + k_specs + v_specs + [vec, vec],
        out_specs=pl.BlockSpec((None, MOBA_BLOCK, hps * hd), lambda b, h, i: (b, i, h)),
        scratch_shapes=[
            pltpu.VMEM((hps, s, hd), BF16),
            pltpu.VMEM((hps, n_blk, hd, MOBA_BLOCK), BF16),
            pltpu.VMEM((hps, n_blk, hd), F32),
            pltpu.VMEM((hps, n_blk, MOBA_BLOCK), F32),
            pltpu.VMEM((hps, n_blk, MOBA_BLOCK, MOBA_BLOCK), F32),
            pltpu.VMEM((hps, MOBA_BLOCK, MOBA_BLOCK), F32),
        ],
        compiler_params=_cparams(("arbitrary", "arbitrary", "arbitrary")),
        name="moba",
    )(*([z3] * (3 * hps)), gq, gk)


MERGE_COLS = 512


def _merge_kernel(a_ref, gl_ref, mo_ref, gt_ref, x_ref, wc_ref, wg_ref, wm_ref, wo_ref, gn_ref,
                  wup_ref, wdn_ref, o_ref, hn_ref, wup_o, wdn_o, mg):
    d = D_MODEL
    wup_o[...] = wup_ref[...].astype(wup_o.dtype)
    wdn_o[...] = wdn_ref[...].astype(wdn_o.dtype)
    for c0 in range(0, d, MERGE_COLS):
        cs = slice(c0, c0 + MERGE_COLS)
        ya = jnp.dot(a_ref[...], wc_ref[:, cs], preferred_element_type=F32)
        yb = jnp.dot(gl_ref[...], wg_ref[:, cs], preferred_element_type=F32)
        yc = jnp.dot(mo_ref[...], wm_ref[:, cs], preferred_element_type=F32)
        g0 = gt_ref[:, c0:c0 + MERGE_COLS].astype(F32)
        g1 = gt_ref[:, d + c0:d + c0 + MERGE_COLS].astype(F32)
        g2 = gt_ref[:, 2 * d + c0:2 * d + c0 + MERGE_COLS].astype(F32)
        mg[:, cs] = (g0 * ya + g1 * yb + g2 * yc).astype(BF16)
    x_new = x_ref[...] + jnp.dot(mg[...], wo_ref[...], preferred_element_type=F32)
    o_ref[...] = x_new
    hn_ref[...] = _rms_rows(x_new, gn_ref[...]).astype(hn_ref.dtype)


def _resident(w_stack, layer):
    return pl.BlockSpec((None,) + w_stack.shape[1:], lambda i: (layer, 0, 0), pipeline_mode=pl.Buffered(1))


def _merge(a_act, o_gla, o_moba, gates, x, wc, wg, wm, wo, g_next, w_up, w_down, layer, *, tm=256):
    m, d = x.shape
    steps = m // tm
    up_rows, dn_rows = w_up.shape[1] // steps, w_down.shape[1] // steps

    def rows(width):
        return pl.BlockSpec((tm, width), lambda i: (i, 0))

    return pl.pallas_call(
        _merge_kernel,
        out_shape=(jax.ShapeDtypeStruct((m, d), F32), jax.ShapeDtypeStruct((m, d), BF16),
                   jax.ShapeDtypeStruct(w_up.shape[1:], BF16), jax.ShapeDtypeStruct(w_down.shape[1:], BF16)),
        grid=(steps,),
        in_specs=[rows(a_act.shape[1]), rows(o_gla.shape[1]), rows(o_moba.shape[1]), rows(gates.shape[1]),
                  rows(d), _resident(wc, layer), _resident(wg, layer), _resident(wm, layer),
                  _resident(wo, layer), _layer_vec(d, layer),
                  pl.BlockSpec((None, up_rows, w_up.shape[2]), lambda i: (layer, i, 0)),
                  pl.BlockSpec((None, dn_rows, w_down.shape[2]), lambda i: (layer, i, 0))],
        out_specs=(rows(d), rows(d),
                   pl.BlockSpec((up_rows, w_up.shape[2]), lambda i: (i, 0)),
                   pl.BlockSpec((dn_rows, w_down.shape[2]), lambda i: (i, 0))),
        scratch_shapes=[pltpu.VMEM((tm, d), BF16)],
        compiler_params=_cparams(("arbitrary",)),
        name="merge",
    )(a_act, o_gla, o_moba, gates, x, wc, wg, wm, wo, g_next, w_up, w_down)


def _shift_rows(u, k, carry):
    rolled = pltpu.roll(u, k, 0)
    head = jnp.where(lax.broadcasted_iota(jnp.int32, carry.shape, 0) < k,
                     pltpu.roll(carry, k, 0), rolled[:SUBLANE, :])
    return jnp.concatenate([head, rolled[SUBLANE:, :]], axis=0)


def _ffn_kernel(h_ref, wug_ref, wuv_ref, cwg_ref, cwv_ref, wd_ref, o_ref, cg, cv, *, tiles_per_seq):
    i = pl.program_id(0)
    j = pl.program_id(1)
    tm = h_ref.shape[0]

    @pl.when(j == 0)
    def _():
        o_ref[...] = jnp.zeros(o_ref.shape, o_ref.dtype)

    @pl.when(i % tiles_per_seq == 0)
    def _():
        cg[j] = jnp.zeros(cg.shape[1:], F32)
        cv[j] = jnp.zeros(cv.shape[1:], F32)

    h = h_ref[...]

    def conv(w_ref, cw_ref, carry_ref):
        u = jnp.dot(h, w_ref[...], preferred_element_type=F32)
        carry = carry_ref[j]
        out = (u * cw_ref[2:3, :] + _shift_rows(u, 1, carry) * cw_ref[1:2, :]
               + _shift_rows(u, 2, carry) * cw_ref[0:1, :])
        carry_ref[j] = u[tm - SUBLANE:, :]
        return out

    ug = conv(wug_ref, cwg_ref, cg)
    uv = conv(wuv_ref, cwv_ref, cv)
    act = (_silu(ug) * uv).astype(BF16)
    o_ref[...] += jnp.dot(act, wd_ref[...], preferred_element_type=F32)


def _ffn(h, w_up, conv_w, w_down, layer, *, seq, tm=1024, tf=512):
    m, d = h.shape
    f = w_down.shape[0]
    nf = f // tf
    return pl.pallas_call(
        functools.partial(_ffn_kernel, tiles_per_seq=seq // tm),
        out_shape=jax.ShapeDtypeStruct((m, d), F32),
        grid=(m // tm, nf),
        in_specs=[
            pl.BlockSpec((tm, d), lambda i, j: (i, 0)),
            pl.BlockSpec((d, tf), lambda i, j: (0, j)),
            pl.BlockSpec((d, tf), lambda i, j: (0, j + nf)),
            pl.BlockSpec((None, FFN_CONV_WIDTH, tf), lambda i, j: (layer, 0, j)),
            pl.BlockSpec((None, FFN_CONV_WIDTH, tf), lambda i, j: (layer, 0, j + nf)),
            pl.BlockSpec((tf, d), lambda i, j: (j, 0)),
        ],
        out_specs=pl.BlockSpec((tm, d), lambda i, j: (i, 0)),
        scratch_shapes=[
            pltpu.VMEM((nf, SUBLANE, tf), F32),
            pltpu.VMEM((nf, SUBLANE, tf), F32),
        ],
        compiler_params=_cparams(("arbitrary", "arbitrary")),
        name="conv_ffn",
    )(h, w_up, w_up, conv_w, conv_w, w_down)


PLE_COLS = 512


def _ple_kernel(x_ref, y_ref, p_ref, g_ref, wg_ref, wp_ref, *rest):
    h_scr = rest[-1]
    o_ref = rest[0] if len(rest) == 2 else rest[1]
    x2 = x_ref[...] + y_ref[...]
    o_ref[...] = x2
    h_scr[...] = _rms_rows(x2, g_ref[...]).astype(h_scr.dtype)
    p_bf = p_ref[...].astype(BF16)
    for c0 in range(0, D_MODEL, PLE_COLS):
        cs = slice(c0, c0 + PLE_COLS)
        gate = _sigmoid(jnp.dot(h_scr[...], wg_ref[:, cs], preferred_element_type=F32))
        emb = jnp.dot(p_bf, wp_ref[:, cs], preferred_element_type=F32)
        o_ref[:, cs] = o_ref[:, cs] + gate * emb
    if len(rest) == 4:
        gn_ref, _, hn_ref, _ = rest
        hn_ref[...] = _rms_rows(o_ref[...], gn_ref[...]).astype(hn_ref.dtype)


def _ple(x, y, p, g_ple, w_gate, w_ple, layer, g_next, *, tm=512):
    m, d = x.shape
    rows = lambda width: pl.BlockSpec((tm, width), lambda i: (i, 0))
    in_specs = [rows(d), rows(d), pl.BlockSpec((None, tm, PLE_DIM), lambda i: (layer, i, 0)),
                _layer_vec(d, layer), _resident(w_gate, layer), _resident(w_ple, layer)]
    args = [x, y, p, g_ple, w_gate, w_ple]
    out_shape = [jax.ShapeDtypeStruct((m, d), F32)]
    out_specs = [rows(d)]
    if g_next is not None:
        in_specs.append(_layer_vec(d, layer + 1))
        args.append(g_next)
        out_shape.append(jax.ShapeDtypeStruct((m, d), BF16))
        out_specs.append(rows(d))
    return pl.pallas_call(
        _ple_kernel,
        out_shape=tuple(out_shape),
        grid=(m // tm,),
        in_specs=in_specs,
        out_specs=tuple(out_specs),
        scratch_shapes=[pltpu.VMEM((tm, d), BF16)],
        compiler_params=_cparams(("arbitrary",)),
        name="ple",
    )(*args)


def _pad_in_proj_kernel(w_ref, o_ref):
    ga_end = OFF_GA + GLA_GATE_RANK
    o_ref[:, :ga_end] = w_ref[:, :ga_end].astype(o_ref.dtype)
    o_ref[:, ga_end:OFF_MQ] = jnp.zeros((o_ref.shape[0], OFF_MQ - ga_end), o_ref.dtype)
    o_ref[:, OFF_MQ:] = w_ref[:, ga_end:].astype(o_ref.dtype)


def _pad_in_proj(w_in, *, tr=256):
    depth, d, n = w_in.shape
    return pl.pallas_call(
        _pad_in_proj_kernel,
        out_shape=jax.ShapeDtypeStruct((depth, d, Z_COLS), BF16),
        grid=(depth, d // tr),
        in_specs=[pl.BlockSpec((None, tr, n), lambda l, r: (l, r, 0))],
        out_specs=pl.BlockSpec((None, tr, Z_COLS), lambda l, r: (l, r, 0)),
        compiler_params=_cparams(("arbitrary", "arbitrary")),
        name="pad_in_proj",
    )(w_in)


def kernel(x, p, norm_mix_g, w_in, conv_w, conv_b, conv_ln_g, conv_ln_b, w_conv_out, gla_w_a2, gla_b_a,
           gla_norm_g, w_gla_out, moba_q_norm_g, moba_k_norm_g, w_moba_out, w_gate, b_gate, w_out,
           norm_ffn_g, w_up, ffn_conv_w, w_down, norm_ple_g, w_ple_gate, w_ple):
    b_sz, s, d = x.shape
    depth = w_in.shape[0]
    m = b_sz * s
    xf = x.reshape(m, d)
    p2 = p.reshape(depth, m, PLE_DIM)
    row = lambda v: v.reshape(1, -1)
    stack_vec = lambda v: v.reshape(depth, 1, -1)
    w_in_p = _pad_in_proj(w_in)
    wc_b, wg_b, wm_b, wo_b = (w.astype(BF16) for w in (w_conv_out, w_gla_out, w_moba_out, w_out))
    w_pg_b, w_ple_b = w_ple_gate.astype(BF16), w_ple.astype(BF16)
    w_a2p = jnp.concatenate(
        [gla_w_a2, jnp.zeros((depth, GATE_PAD - GLA_GATE_RANK, GLA_QK), F32)], axis=1).astype(BF16)
    g_mix, g_ffn, g_ple, b_gate3 = (stack_vec(v) for v in (norm_mix_g, norm_ffn_g, norm_ple_g, b_gate))
    conv_b3, conv_ln_g3, conv_ln_b3 = (stack_vec(v) for v in (conv_b, conv_ln_g, conv_ln_b))

    h = _rmsnorm(xf, g_mix, 0)
    for i in range(depth):
        z = _proj(h, w_in_p, None, i, tm=2048, tn=1152, name="in_proj")
        gates = _proj(h, w_gate, b_gate3, i, tm=2048, tn=768, name="gate_proj")
        z3 = z.reshape(b_sz, s, Z_COLS)
        a_act = _conformer_conv(z3, conv_w, conv_b3, conv_ln_g3, conv_ln_b3, i)
        o_gla = _gla(z3, w_a2p[i], row(gla_b_a[i]), row(gla_norm_g[i]))
        o_moba = _moba(z3, row(moba_q_norm_g[i]), row(moba_k_norm_g[i]))
        xf, h, w_up_b, w_down_b = _merge(a_act.reshape(m, -1), o_gla.reshape(m, -1), o_moba.reshape(m, -1),
                                         gates, xf, wc_b, wg_b, wm_b, wo_b, g_ffn, w_up, w_down, i)
        y = _ffn(h, w_up_b, ffn_conv_w, w_down_b, i, seq=s)
        if i + 1 < depth:
            xf, h = _ple(xf, y, p2, g_ple, w_pg_b, w_ple_b, i, g_mix)
        else:
            (xf,) = _ple(xf, y, p2, g_ple, w_pg_b, w_ple_b, i, None)
    return xf.reshape(b_sz, s, d)
```

```python
import functools

import jax
import jax.numpy as jnp
from jax import lax
from jax.experimental import pallas as pl
from jax.experimental.pallas import tpu as pltpu

F32 = jnp.float32
BF16 = jnp.bfloat16

D_MODEL = 2048
PLE_DIM = 256
EPS = 1e-6
CONV_CH = 512
CONV_WIDTH = 31
GLA_HEADS = 4
GLA_DV = 128
GLA_DK = 64
GLA_GATE_RANK = 16
GLA_TAU = 16.0
GLA_CHUNK = 64
GLA_SUB = 16
MOBA_HEADS = 8
MOBA_HD = 128
MOBA_BLOCK = 256
MOBA_TOPK = 3
FFN_DIM = 5632
FFN_CONV_WIDTH = 3

LANE = 128
SUBLANE = 8
GATE_PAD = LANE
OFF_GLU = 0
OFF_GQ = OFF_GLU + 2 * CONV_CH
OFF_GK = OFF_GQ + GLA_HEADS * GLA_DK
OFF_GV = OFF_GK + GLA_HEADS * GLA_DK
OFF_GR = OFF_GV + GLA_HEADS * GLA_DV
W_CONV_GLA = OFF_GR + GLA_HEADS * GLA_DV
OFF_MQ = 0
OFF_MK = OFF_MQ + MOBA_HEADS * MOBA_HD
OFF_MV = OFF_MK + MOBA_HEADS * MOBA_HD
W_MOBA = OFF_MV + MOBA_HEADS * MOBA_HD
NT_DIMS = (((1,), (1,)), ((), ()))

VMEM_LIMIT = 56 * 1024 * 1024


def _cparams(sem):
    return pltpu.CompilerParams(dimension_semantics=sem, vmem_limit_bytes=VMEM_LIMIT)


def _sigmoid(x):
    return 1.0 / (1.0 + jnp.exp(-x))


def _silu(x):
    return x * _sigmoid(x)


def _rms_rows(x, g):
    ms = jnp.mean(x * x, axis=-1, keepdims=True)
    return x * lax.rsqrt(ms + EPS) * g


def _layer_vec(width, layer):
    return pl.BlockSpec((None, 1, width), lambda i: (layer, 0, 0))


def _rmsnorm_kernel(x_ref, g_ref, o_ref):
    o_ref[...] = _rms_rows(x_ref[...], g_ref[...]).astype(o_ref.dtype)


def _rmsnorm(x, g_stack, layer, *, tm=1024):
    m, d = x.shape
    return pl.pallas_call(
        _rmsnorm_kernel,
        out_shape=jax.ShapeDtypeStruct((m, d), BF16),
        grid=(m // tm,),
        in_specs=[pl.BlockSpec((tm, d), lambda i: (i, 0)), _layer_vec(d, layer)],
        out_specs=pl.BlockSpec((tm, d), lambda i: (i, 0)),
        compiler_params=_cparams(("arbitrary",)),
        name="rmsnorm",
    )(x, g_stack)


def _proj_nt_kernel(h_ref, wt_ref, o_ref):
    acc = lax.dot_general(h_ref[...], wt_ref[0].astype(BF16), NT_DIMS, preferred_element_type=F32)
    o_ref[...] = acc.astype(o_ref.dtype)


def _proj_nt(h, wt_stack, layer, row0, n, *, tm, tn, name):
    m, d = h.shape
    return pl.pallas_call(
        _proj_nt_kernel,
        out_shape=jax.ShapeDtypeStruct((m, n), BF16),
        grid=(m // tm, n // tn),
        in_specs=[pl.BlockSpec((tm, d), lambda i, j: (i, 0)),
                  pl.BlockSpec((pl.Element(1), pl.Element(tn), pl.Element(d)),
                               lambda i, j: (layer, pl.multiple_of(row0 + j * tn, SUBLANE), 0))],
        out_specs=pl.BlockSpec((tm, tn), lambda i, j: (i, j)),
        compiler_params=_cparams(("arbitrary", "arbitrary")),
        name=name,
    )(h, wt_stack)


def _proj_sigmoid_kernel(h_ref, w_ref, b_ref, o_ref):
    acc = jnp.dot(h_ref[...], w_ref[...].astype(BF16), preferred_element_type=F32)
    o_ref[...] = _sigmoid(acc + b_ref[...]).astype(o_ref.dtype)


def _proj_sigmoid(h, w_stack, b_stack, layer, *, tm, tn, name):
    m, d = h.shape
    n = w_stack.shape[2]
    return pl.pallas_call(
        _proj_sigmoid_kernel,
        out_shape=jax.ShapeDtypeStruct((m, n), BF16),
        grid=(m // tm, n // tn),
        in_specs=[pl.BlockSpec((tm, d), lambda i, j: (i, 0)),
                  pl.BlockSpec((None, d, tn), lambda i, j: (layer, 0, j)),
                  pl.BlockSpec((None, 1, tn), lambda i, j: (layer, 0, j))],
        out_specs=pl.BlockSpec((tm, tn), lambda i, j: (i, j)),
        compiler_params=_cparams(("arbitrary", "arbitrary")),
        name=name,
    )(h, w_stack, b_stack)


CONV_HALO = 32
CONV_ROWS = 64


def _conv_kernel(z_ref, w_ref, cb_ref, lg_ref, lb_ref, o_ref, abuf, sh, *, ts):
    t = pl.program_id(1)

    @pl.when(t == 0)
    def _():
        abuf[0:CONV_HALO, :] = jnp.zeros((CONV_HALO, CONV_CH), F32)

    @pl.when(t > 0)
    def _():
        abuf[0:CONV_HALO, :] = abuf[ts:ts + CONV_HALO, :]

    z = z_ref[...].astype(F32)
    abuf[CONV_HALO:CONV_HALO + ts, :] = z[:, :CONV_CH] * _sigmoid(z[:, CONV_CH:])

    sh_rows = sh.shape[1]
    for b in range(1, SUBLANE):
        sh[b] = abuf[b:b + sh_rows, :]

    first = CONV_HALO - (CONV_WIDTH - 1)
    for r0 in range(0, ts, CONV_ROWS):
        acc = abuf[r0 + CONV_HALO:r0 + CONV_HALO + CONV_ROWS, :] * w_ref[CONV_WIDTH - 1:CONV_WIDTH, :]
        for j in range(CONV_WIDTH - 1):
            res = (first + j) % SUBLANE
            base = r0 + first + j - res
            rows = abuf[base:base + CONV_ROWS, :] if res == 0 else sh[res, base:base + CONV_ROWS, :]
            acc = acc + rows * w_ref[j:j + 1, :]
        acc = acc + cb_ref[...]
        mu = jnp.mean(acc, axis=-1, keepdims=True)
        xc = acc - mu
        var = jnp.mean(xc * xc, axis=-1, keepdims=True)
        y = xc * lax.rsqrt(var + EPS) * lg_ref[...] + lb_ref[...]
        o_ref[r0:r0 + CONV_ROWS, :] = _silu(y).astype(o_ref.dtype)


def _conformer_conv(z3, conv_w, conv_b, ln_g, ln_b, layer, *, ts=512):
    b_sz, s, _ = z3.shape

    def par(a):
        return pl.BlockSpec((None,) + a.shape[1:], lambda b, t: (layer, 0, 0))

    return pl.pallas_call(
        functools.partial(_conv_kernel, ts=ts),
        out_shape=jax.ShapeDtypeStruct((b_sz, s, CONV_CH), BF16),
        grid=(b_sz, s // ts),
        in_specs=[
            pl.BlockSpec((None, ts, 2 * CONV_CH), lambda b, t: (b, t, OFF_GLU // (2 * CONV_CH))),
            par(conv_w), par(conv_b), par(ln_g), par(ln_b),
        ],
        out_specs=pl.BlockSpec((None, ts, CONV_CH), lambda b, t: (b, t, 0)),
        scratch_shapes=[pltpu.VMEM((CONV_HALO + ts, CONV_CH), F32),
                        pltpu.VMEM((SUBLANE, CONV_HALO + ts - SUBLANE, CONV_CH), F32)],
        compiler_params=_cparams(("arbitrary", "arbitrary")),
        name="conformer_conv",
    )(z3, conv_w, conv_b, ln_g, ln_b)


LOG2_E = 1.4426950408889634
GLA_QK = GLA_HEADS * GLA_DK
GLA_V = GLA_HEADS * GLA_DV


GLA_CPS = 4


def _gla_kernel(q_ref, k_ref, v_ref, r_ref, al_ref, wa_ref, ba_ref, ng_ref, o_ref,
                st, qf, kf, vf, bcs, pbig, intra):
    @pl.when(pl.program_id(1) == 0)
    def _():
        st[...] = jnp.zeros(st.shape, F32)

    for ci in range(GLA_CPS):
        rs = slice(ci * GLA_CHUNK, (ci + 1) * GLA_CHUNK)
        _gla_chunk(q_ref.at[rs, :], k_ref.at[rs, :], v_ref.at[rs, :], r_ref.at[rs, :], al_ref.at[rs, :],
                   wa_ref, ba_ref, ng_ref, o_ref.at[rs, :],
                   st, qf.at[ci], kf.at[ci], vf.at[ci], bcs.at[ci], pbig.at[ci], intra.at[ci])


def _gla_chunk(q_ref, k_ref, v_ref, r_ref, al_ref, wa_ref, ba_ref, ng_ref, o_ref,
               st, qf, kf, vf, bcs, pbig, intra):
    c = GLA_CHUNK

    xa = jnp.dot(al_ref[...], wa_ref[...], preferred_element_type=F32) + ba_ref[...]
    la = (jnp.minimum(xa, 0.0) - jnp.log(1.0 + jnp.exp(-jnp.abs(xa)))) * (1.0 / GLA_TAU)
    row = lax.broadcasted_iota(jnp.int32, (c, c), 0)
    col = lax.broadcasted_iota(jnp.int32, (c, c), 1)
    tri = jnp.where(row >= col, 1.0, 0.0).astype(BF16)
    la_hi = la.astype(BF16)
    la_lo = (la - la_hi.astype(F32)).astype(BF16)
    bc = (jnp.dot(tri, la_hi, preferred_element_type=F32)
          + jnp.dot(tri, la_lo, preferred_element_type=F32))

    q = q_ref[...].astype(F32) * (GLA_DK ** -0.5)
    k = k_ref[...].astype(F32)
    qf[...] = q
    kf[...] = k
    vf[...] = v_ref[...].astype(F32)
    bcs[...] = bc * LOG2_E

    b_last = bc[c - 1:c, :]
    q_in = (q * jnp.exp(bc)).astype(BF16)
    k_dec = (k * jnp.exp(b_last - bc)).astype(BF16)
    e_last = jnp.exp(b_last)
    v_bf = v_ref[...]

    srow = lax.broadcasted_iota(jnp.int32, (GLA_QK, GLA_V), 0) // GLA_DK
    scol = lax.broadcasted_iota(jnp.int32, (GLA_QK, GLA_V), 1) // GLA_DV
    sel = jnp.where(srow == scol, 1.0, 0.0).astype(BF16)

    for j in range(c // GLA_SUB):
        t0 = j * GLA_SUB
        tj = c - t0
        q_rows = qf[t0:c, :]
        bc_rows = bcs[t0:c, :]
        t_loc = lax.broadcasted_iota(jnp.int32, (GLA_SUB, GLA_QK), 0)
        for sl in range(GLA_SUB):
            s = t0 + sl
            d = bc_rows - bcs[s:s + 1, :]
            d_own = jnp.minimum(d[:GLA_SUB, :], 0.0)
            d = d_own if tj == GLA_SUB else jnp.concatenate([d_own, d[GLA_SUB:, :]], axis=0)
            p = (q_rows * kf[s:s + 1, :]) * jnp.exp2(d)
            p_own = jnp.where(t_loc >= sl, p[:GLA_SUB, :], 0.0)
            p = p_own if tj == GLA_SUB else jnp.concatenate([p_own, p[GLA_SUB:, :]], axis=0)
            pbig[sl * tj:(sl + 1) * tj, :] = p.astype(BF16)
        r_all = jnp.dot(pbig[0:GLA_SUB * tj, :], sel, preferred_element_type=F32)
        acc = jnp.zeros((tj, GLA_V), F32)
        for sl in range(GLA_SUB):
            s = t0 + sl
            acc = acc + r_all[sl * tj:(sl + 1) * tj, :] * vf[s:s + 1, :]
        if j == 0:
            intra[...] = acc
        else:
            intra[t0:c, :] = intra[t0:c, :] + acc

    for h in range(GLA_HEADS):
        ks = slice(h * GLA_DK, (h + 1) * GLA_DK)
        vs = slice(h * GLA_DV, (h + 1) * GLA_DV)
        st_h = st[h]
        inter = lax.dot_general(q_in[:, ks], st_h.astype(BF16), (((1,), (1,)), ((), ())),
                                preferred_element_type=F32)
        upd = lax.dot_general(v_bf[:, vs], k_dec[:, ks], (((0,), (0,)), ((), ())),
                              preferred_element_type=F32)
        st[h] = st_h * e_last[:, ks] + upd
        o_h = inter + intra[:, vs]
        y = _rms_rows(o_h, ng_ref[...])
        o_ref[:, vs] = (y * _silu(r_ref[:, vs].astype(F32))).astype(o_ref.dtype)


def _gla(z3, zg3, w_a2p, b_a, norm_g):
    b_sz, s, _ = z3.shape
    c = GLA_CHUNK
    rows = c * GLA_CPS
    cps = GLA_CPS

    def zspec(width, off):
        return pl.BlockSpec((None, rows, width), lambda b, t: (b, t, off // width))

    return pl.pallas_call(
        _gla_kernel,
        out_shape=jax.ShapeDtypeStruct((b_sz, s, GLA_V), BF16),
        grid=(b_sz, s // rows),
        in_specs=[
            zspec(GLA_QK, OFF_GQ), zspec(GLA_QK, OFF_GK), zspec(GLA_V, OFF_GV), zspec(GLA_V, OFF_GR),
            zspec(GATE_PAD, 0),
            pl.BlockSpec((GATE_PAD, GLA_QK), lambda b, t: (0, 0)),
            pl.BlockSpec((1, GLA_QK), lambda b, t: (0, 0)),
            pl.BlockSpec((1, GLA_DV), lambda b, t: (0, 0)),
        ],
        out_specs=pl.BlockSpec((None, rows, GLA_V), lambda b, t: (b, t, 0)),
        scratch_shapes=[
            pltpu.VMEM((GLA_HEADS, GLA_DV, GLA_DK), F32),
            pltpu.VMEM((cps, c, GLA_QK), F32),
            pltpu.VMEM((cps, c, GLA_QK), F32),
            pltpu.VMEM((cps, c, GLA_V), F32),
            pltpu.VMEM((cps, c, GLA_QK), F32),
            pltpu.VMEM((cps, GLA_SUB * c, GLA_QK), BF16),
            pltpu.VMEM((cps, c, GLA_V), F32),
        ],
        compiler_params=_cparams(("arbitrary", "arbitrary")),
        name="gla",
    )(z3, z3, z3, z3, zg3, w_a2p, b_a, norm_g)


NEG_INF = float("-inf")
MOBA_GROUP = 4


def _fold_rows(x, op):
    return op(x.reshape(x.shape[0] // SUBLANE, SUBLANE, x.shape[1]), axis=0)


MOBA_HPS = 4


def _moba_kernel(*refs, n_blk):
    hps = MOBA_HPS
    q_refs, k_refs, v_refs = refs[:hps], refs[hps:2 * hps], refs[2 * hps:3 * hps]
    gq_ref, gk_ref, o_ref, kn, vt, kmean, chosen, s_scr, s_own = refs[3 * hps:]
    qi = pl.program_id(2)
    blk = MOBA_BLOCK
    grp = MOBA_GROUP
    heads = range(hps)

    @pl.when(qi == 0)
    def _():
        for hh in heads:
            for n in range(n_blk):
                kb = _rms_rows(k_refs[hh][n * blk:(n + 1) * blk, :].astype(F32), gk_ref[...])
                kn[hh, n * blk:(n + 1) * blk, :] = kb.astype(BF16)
                kmean[hh, n:n + 1, :] = jnp.mean(kb, axis=0, keepdims=True)
                vt[hh, n] = v_refs[hh][n * blk:(n + 1) * blk, :].astype(F32).T.astype(BF16)

    brow = lax.broadcasted_iota(jnp.int32, (n_blk, blk), 0)
    past = brow < qi
    k_i = lax.broadcasted_iota(jnp.int32, (blk, blk), 0)
    q_i = lax.broadcasted_iota(jnp.int32, (blk, blk), 1)
    own = pl.multiple_of(qi * blk, blk)
    q_bf, m_run0 = [], []
    for hh in heads:
        qn = _rms_rows(q_refs[hh][...].astype(F32), gq_ref[...])
        q_bf.append((qn * (MOBA_HD ** -0.5 * LOG2_E)).astype(BF16))
        sb = lax.dot_general(kmean[hh], qn, NT_DIMS, precision=lax.Precision.HIGHEST,
                             preferred_element_type=F32)
        sb = jnp.where(past, sb, NEG_INF)
        rank = jnp.zeros((n_blk, blk), F32)
        for m in range(n_blk):
            cm = sb[m:m + 1, :]
            ge = jnp.where(cm >= sb, 1.0, 0.0)
            gt = jnp.where(cm > sb, 1.0, 0.0)
            rank = rank + jnp.where(brow > m, ge, gt)
        chosen[hh] = jnp.where(past, jnp.where(rank < float(MOBA_TOPK), 1.0, 0.0), 0.0)
        so = lax.dot_general(kn[hh, pl.ds(own, blk), :], q_bf[hh], NT_DIMS, preferred_element_type=F32)
        so = jnp.where(k_i <= q_i, so, NEG_INF)
        s_own[hh] = so
        m_run0.append(_fold_rows(so, jnp.max))
    n_grp = (qi + grp - 1) // grp

    def scores(g, m_runs):
        start = pl.multiple_of(g * (grp * blk), grp * blk)
        out = []
        for hh in heads:
            m_run = m_runs[hh]
            sg = lax.dot_general(kn[hh, pl.ds(start, grp * blk), :], q_bf[hh], NT_DIMS,
                                 preferred_element_type=F32)
            for b in range(grp):
                n = g * grp + b
                sn = jnp.where(chosen[hh, pl.ds(n, 1), :] > 0.0, sg[b * blk:(b + 1) * blk, :], NEG_INF)
                s_scr[hh, n] = sn
                m_run = jnp.maximum(m_run, _fold_rows(sn, jnp.max))
            out.append(m_run)
        return tuple(out)

    m_runs = lax.fori_loop(0, n_grp, scores, tuple(m_run0))
    m_rows = [jnp.max(m_runs[hh], axis=0, keepdims=True) for hh in heads]

    carry0 = []
    for hh in heads:
        p_own = jnp.exp2(s_own[hh] - m_rows[hh])
        carry0.append(_fold_rows(p_own, jnp.sum))
        carry0.append(jnp.dot(vt[hh, qi], p_own.astype(BF16), preferred_element_type=F32))

    def values(g, carry):
        out = []
        for hh in heads:
            l_run, acc = carry[2 * hh], carry[2 * hh + 1]
            ps, vs = [], []
            for b in range(grp):
                n = g * grp + b
                pn = jnp.exp2(s_scr[hh, n] - m_rows[hh])
                l_run = l_run + _fold_rows(pn, jnp.sum)
                ps.append(pn.astype(BF16))
                vs.append(vt[hh, n])
            acc = acc + jnp.dot(jnp.concatenate(vs, axis=1), jnp.concatenate(ps, axis=0),
                                preferred_element_type=F32)
            out += [l_run, acc]
        return tuple(out)

    carry = lax.fori_loop(0, n_grp, values, tuple(carry0))
    for hh in heads:
        l_row = jnp.sum(carry[2 * hh], axis=0, keepdims=True)
        o_ref[:, hh * MOBA_HD:(hh + 1) * MOBA_HD] = (carry[2 * hh + 1] / l_row).T.astype(o_ref.dtype)


def _moba(z3, gq, gk):
    b_sz, s, _ = z3.shape
    n_blk = s // MOBA_BLOCK
    hd = MOBA_HD
    hps = MOBA_HPS

    def head_spec(rows, off, hh, row_index):
        return pl.BlockSpec((None, rows, hd), lambda b, h, i: (b, row_index(i), off // hd + h * hps + hh))

    q_specs = [head_spec(MOBA_BLOCK, OFF_MQ, hh, lambda i: i) for hh in range(hps)]
    k_specs = [head_spec(s, OFF_MK, hh, lambda i: 0) for hh in range(hps)]
    v_specs = [head_spec(s, OFF_MV, hh, lambda i: 0) for hh in range(hps)]
    vec = pl.BlockSpec((1, hd), lambda b, h, i: (0, 0))
    return pl.pallas_call(
        functools.partial(_moba_kernel, n_blk=n_blk),
        out_shape=jax.ShapeDtypeStruct((b_sz, s, MOBA_HEADS * hd), BF16),
        grid=(b_sz, MOBA_HEADS // hps, n_blk),
        in_specs=q_specs + k_specs + v_specs + [vec, vec],
        out_specs=pl.BlockSpec((None, MOBA_BLOCK, hps * hd), lambda b, h, i: (b, i, h)),
        scratch_shapes=[
            pltpu.VMEM((hps, s, hd), BF16),
            pltpu.VMEM((hps, n_blk, hd, MOBA_BLOCK), BF16),
            pltpu.VMEM((hps, n_blk, hd), F32),
            pltpu.VMEM((hps, n_blk, MOBA_BLOCK), F32),
            pltpu.VMEM((hps, n_blk, MOBA_BLOCK, MOBA_BLOCK), F32),
            pltpu.VMEM((hps, MOBA_BLOCK, MOBA_BLOCK), F32),
        ],
        compiler_params=_cparams(("arbitrary", "arbitrary", "arbitrary")),
        name="moba",
    )(*([z3] * (3 * hps)), gq, gk)


MERGE_COLS = 512


def _merge_kernel(a_ref, gl_ref, mo_ref, gt_ref, x_ref, wc_ref, wg_ref, wm_ref, wo_ref, gn_ref,
                  wup_ref, wdn_ref, o_ref, hn_ref, wup_o, wdn_o, mg):
    d = D_MODEL
    wup_o[...] = wup_ref[...].astype(wup_o.dtype)
    wdn_o[...] = wdn_ref[...].astype(wdn_o.dtype)
    for c0 in range(0, d, MERGE_COLS):
        cs = slice(c0, c0 + MERGE_COLS)
        ya = jnp.dot(a_ref[...], wc_ref[:, cs], preferred_element_type=F32)
        yb = jnp.dot(gl_ref[...], wg_ref[:, cs], preferred_element_type=F32)
        yc = jnp.dot(mo_ref[...], wm_ref[:, cs], preferred_element_type=F32)
        g0 = gt_ref[:, c0:c0 + MERGE_COLS].astype(F32)
        g1 = gt_ref[:, d + c0:d + c0 + MERGE_COLS].astype(F32)
        g2 = gt_ref[:, 2 * d + c0:2 * d + c0 + MERGE_COLS].astype(F32)
        mg[:, cs] = (g0 * ya + g1 * yb + g2 * yc).astype(BF16)
    x_new = x_ref[...] + jnp.dot(mg[...], wo_ref[...], preferred_element_type=F32)
    o_ref[...] = x_new
    hn_ref[...] = _rms_rows(x_new, gn_ref[...]).astype(hn_ref.dtype)


def _resident(w_stack, layer):
    return pl.BlockSpec((None,) + w_stack.shape[1:], lambda i: (layer, 0, 0), pipeline_mode=pl.Buffered(1))


def _merge(a_act, o_gla, o_moba, gates, x, wc, wg, wm, wo, g_next, w_up, w_down, layer, *, tm=256):
    m, d = x.shape
    steps = m // tm
    up_rows, dn_rows = w_up.shape[1] // steps, w_down.shape[1] // steps

    def rows(width):
        return pl.BlockSpec((tm, width), lambda i: (i, 0))

    return pl.pallas_call(
        _merge_kernel,
        out_shape=(jax.ShapeDtypeStruct((m, d), F32), jax.ShapeDtypeStruct((m, d), BF16),
                   jax.ShapeDtypeStruct(w_up.shape[1:], BF16), jax.ShapeDtypeStruct(w_down.shape[1:], BF16)),
        grid=(steps,),
        in_specs=[rows(a_act.shape[1]), rows(o_gla.shape[1]), rows(o_moba.shape[1]), rows(gates.shape[1]),
                  rows(d), _resident(wc, layer), _resident(wg, layer), _resident(wm, layer),
                  _resident(wo, layer), _layer_vec(d, layer),
                  pl.BlockSpec((None, up_rows, w_up.shape[2]), lambda i: (layer, i, 0)),
                  pl.BlockSpec((None, dn_rows, w_down.shape[2]), lambda i: (layer, i, 0))],
        out_specs=(rows(d), rows(d),
                   pl.BlockSpec((up_rows, w_up.shape[2]), lambda i: (i, 0)),
                   pl.BlockSpec((dn_rows, w_down.shape[2]), lambda i: (i, 0))),
        scratch_shapes=[pltpu.VMEM((tm, d), BF16)],
        compiler_params=_cparams(("arbitrary",)),
        name="merge",
    )(a_act, o_gla, o_moba, gates, x, wc, wg, wm, wo, g_next, w_up, w_down)


def _shift_rows(u, k, carry):
    rolled = pltpu.roll(u, k, 0)
    head = jnp.where(lax.broadcasted_iota(jnp.int32, carry.shape, 0) < k,
                     pltpu.roll(carry, k, 0), rolled[:SUBLANE, :])
    return jnp.concatenate([head, rolled[SUBLANE:, :]], axis=0)


def _ffn_kernel(h_ref, wug_ref, wuv_ref, cwg_ref, cwv_ref, wd_ref, o_ref, cg, cv, *, tiles_per_seq):
    i = pl.program_id(0)
    j = pl.program_id(1)
    tm = h_ref.shape[0]

    @pl.when(j == 0)
    def _():
        o_ref[...] = jnp.zeros(o_ref.shape, o_ref.dtype)

    @pl.when(i % tiles_per_seq == 0)
    def _():
        cg[j] = jnp.zeros(cg.shape[1:], F32)
        cv[j] = jnp.zeros(cv.shape[1:], F32)

    h = h_ref[...]

    def conv(w_ref, cw_ref, carry_ref):
        u = jnp.dot(h, w_ref[...], preferred_element_type=F32)
        carry = carry_ref[j]
        out = (u * cw_ref[2:3, :] + _shift_rows(u, 1, carry) * cw_ref[1:2, :]
               + _shift_rows(u, 2, carry) * cw_ref[0:1, :])
        carry_ref[j] = u[tm - SUBLANE:, :]
        return out

    ug = conv(wug_ref, cwg_ref, cg)
    uv = conv(wuv_ref, cwv_ref, cv)
    act = (_silu(ug) * uv).astype(BF16)
    o_ref[...] += jnp.dot(act, wd_ref[...], preferred_element_type=F32)


def _ffn(h, w_up, conv_w, w_down, layer, *, seq, tm=1024, tf=512):
    m, d = h.shape
    f = w_down.shape[0]
    nf = f // tf
    return pl.pallas_call(
        functools.partial(_ffn_kernel, tiles_per_seq=seq // tm),
        out_shape=jax.ShapeDtypeStruct((m, d), F32),
        grid=(m // tm, nf),
        in_specs=[
            pl.BlockSpec((tm, d), lambda i, j: (i, 0)),
            pl.BlockSpec((d, tf), lambda i, j: (0, j)),
            pl.BlockSpec((d, tf), lambda i, j: (0, j + nf)),
            pl.BlockSpec((None, FFN_CONV_WIDTH, tf), lambda i, j: (layer, 0, j)),
            pl.BlockSpec((None, FFN_CONV_WIDTH, tf), lambda i, j: (layer, 0, j + nf)),
            pl.BlockSpec((tf, d), lambda i, j: (j, 0)),
        ],
        out_specs=pl.BlockSpec((tm, d), lambda i, j: (i, 0)),
        scratch_shapes=[
            pltpu.VMEM((nf, SUBLANE, tf), F32),
            pltpu.VMEM((nf, SUBLANE, tf), F32),
        ],
        compiler_params=_cparams(("arbitrary", "arbitrary")),
        name="conv_ffn",
    )(h, w_up, w_up, conv_w, conv_w, w_down)


PLE_COLS = 512


def _ple_kernel(x_ref, y_ref, p_ref, g_ref, wg_ref, wp_ref, *rest):
    h_scr = rest[-1]
    o_ref = rest[0] if len(rest) == 2 else rest[1]
    x2 = x_ref[...] + y_ref[...]
    o_ref[...] = x2
    h_scr[...] = _rms_rows(x2, g_ref[...]).astype(h_scr.dtype)
    p_bf = p_ref[...].astype(BF16)
    for c0 in range(0, D_MODEL, PLE_COLS):
        cs = slice(c0, c0 + PLE_COLS)
        gate = _sigmoid(jnp.dot(h_scr[...], wg_ref[:, cs], preferred_element_type=F32))
        emb = jnp.dot(p_bf, wp_ref[:, cs], preferred_element_type=F32)
        o_ref[:, cs] = o_ref[:, cs] + gate * emb
    if len(rest) == 4:
        gn_ref, _, hn_ref, _ = rest
        hn_ref[...] = _rms_rows(o_ref[...], gn_ref[...]).astype(hn_ref.dtype)


def _ple(x, y, p, g_ple, w_gate, w_ple, layer, g_next, *, tm=512):
    m, d = x.shape
    rows = lambda width: pl.BlockSpec((tm, width), lambda i: (i, 0))
    in_specs = [rows(d), rows(d), pl.BlockSpec((None, tm, PLE_DIM), lambda i: (layer, i, 0)),
                _layer_vec(d, layer), _resident(w_gate, layer), _resident(w_ple, layer)]
    args = [x, y, p, g_ple, w_gate, w_ple]
    out_shape = [jax.ShapeDtypeStruct((m, d), F32)]
    out_specs = [rows(d)]
    if g_next is not None:
        in_specs.append(_layer_vec(d, layer + 1))
        args.append(g_next)
        out_shape.append(jax.ShapeDtypeStruct((m, d), BF16))
        out_specs.append(rows(d))
    return pl.pallas_call(
        _ple_kernel,
        out_shape=tuple(out_shape),
        grid=(m // tm,),
        in_specs=in_specs,
        out_specs=tuple(out_specs),
        scratch_shapes=[pltpu.VMEM((tm, d), BF16)],
        compiler_params=_cparams(("arbitrary",)),
        name="ple",
    )(*args)


def kernel(x, p, norm_mix_g, w_in, conv_w, conv_b, conv_ln_g, conv_ln_b, w_conv_out, gla_w_a2, gla_b_a,
           gla_norm_g, w_gla_out, moba_q_norm_g, moba_k_norm_g, w_moba_out, w_gate, b_gate, w_out,
           norm_ffn_g, w_up, ffn_conv_w, w_down, norm_ple_g, w_ple_gate, w_ple):
    b_sz, s, d = x.shape
    depth = w_in.shape[0]
    m = b_sz * s
    xf = x.reshape(m, d)
    p2 = p.reshape(depth, m, PLE_DIM)
    row = lambda v: v.reshape(1, -1)
    stack_vec = lambda v: v.reshape(depth, 1, -1)
    w_in_t = jnp.swapaxes(w_in, 1, 2)
    wc_b, wg_b, wm_b, wo_b = (w.astype(BF16) for w in (w_conv_out, w_gla_out, w_moba_out, w_out))
    w_pg_b, w_ple_b = w_ple_gate.astype(BF16), w_ple.astype(BF16)
    w_a2p = jnp.concatenate(
        [gla_w_a2, jnp.zeros((depth, GATE_PAD - GLA_GATE_RANK, GLA_QK), F32)], axis=1).astype(BF16)
    g_mix, g_ffn, g_ple, b_gate3 = (stack_vec(v) for v in (norm_mix_g, norm_ffn_g, norm_ple_g, b_gate))
    conv_b3, conv_ln_g3, conv_ln_b3 = (stack_vec(v) for v in (conv_b, conv_ln_g, conv_ln_b))

    h = _rmsnorm(xf, g_mix, 0)
    for i in range(depth):
        z3 = _proj_nt(h, w_in_t, i, 0, W_CONV_GLA, tm=2048, tn=512,
                      name="in_proj_conv_gla").reshape(b_sz, s, W_CONV_GLA)
        zg3 = _proj_nt(h, w_in_t, i, W_CONV_GLA, GATE_PAD, tm=2048, tn=GATE_PAD,
                       name="in_proj_gate").reshape(b_sz, s, GATE_PAD)
        zm3 = _proj_nt(h, w_in_t, i, W_CONV_GLA + GLA_GATE_RANK, W_MOBA, tm=2048, tn=1024,
                       name="in_proj_moba").reshape(b_sz, s, W_MOBA)
        gates = _proj_sigmoid(h, w_gate, b_gate3, i, tm=2048, tn=1024, name="gate_proj")
        a_act = _conformer_conv(z3, conv_w, conv_b3, conv_ln_g3, conv_ln_b3, i)
        o_gla = _gla(z3, zg3, w_a2p[i], row(gla_b_a[i]), row(gla_norm_g[i]))
        o_moba = _moba(zm3, row(moba_q_norm_g[i]), row(moba_k_norm_g[i]))
        xf, h, w_up_b, w_down_b = _merge(a_act.reshape(m, -1), o_gla.reshape(m, -1), o_moba.reshape(m, -1),
                                         gates, xf, wc_b, wg_b, wm_b, wo_b, g_ffn, w_up, w_down, i)
        y = _ffn(h, w_up_b, ffn_conv_w, w_down_b, i, seq=s)
        if i + 1 < depth:
            xf, h = _ple(xf, y, p2, g_ple, w_pg_b, w_ple_b, i, g_mix)
        else:
            (xf,) = _ple(xf, y, p2, g_ple, w_pg_b, w_ple_b, i, None)
    return xf.reshape(b_sz, s, d)
```

```python
import functools

import jax
import jax.numpy as jnp
from jax import lax
from jax.experimental import pallas as pl
from jax.experimental.pallas import tpu as pltpu

F32 = jnp.float32
BF16 = jnp.bfloat16

D_MODEL = 2048
PLE_DIM = 256
EPS = 1e-6
CONV_CH = 512
CONV_WIDTH = 31
GLA_HEADS = 4
GLA_DV = 128
GLA_DK = 64
GLA_GATE_RANK = 16
GLA_TAU = 16.0
GLA_CHUNK = 64
GLA_SUB = 16
MOBA_HEADS = 8
MOBA_HD = 128
MOBA_BLOCK = 256
MOBA_TOPK = 3
FFN_DIM = 5632
FFN_CONV_WIDTH = 3

LANE = 128
SUBLANE = 8
GATE_PAD = LANE
OFF_GLU = 0
OFF_GQ = OFF_GLU + 2 * CONV_CH
OFF_GK = OFF_GQ + GLA_HEADS * GLA_DK
OFF_GV = OFF_GK + GLA_HEADS * GLA_DK
OFF_GR = OFF_GV + GLA_HEADS * GLA_DV
W_CONV_GLA = OFF_GR + GLA_HEADS * GLA_DV
OFF_MQ = 0
OFF_MK = OFF_MQ + MOBA_HEADS * MOBA_HD
OFF_MV = OFF_MK + MOBA_HEADS * MOBA_HD
W_MOBA = OFF_MV + MOBA_HEADS * MOBA_HD
NT_DIMS = (((1,), (1,)), ((), ()))

VMEM_LIMIT = 56 * 1024 * 1024


def _cparams(sem):
    return pltpu.CompilerParams(dimension_semantics=sem, vmem_limit_bytes=VMEM_LIMIT)


def _sigmoid(x):
    return 1.0 / (1.0 + jnp.exp(-x))


def _silu(x):
    return x * _sigmoid(x)


def _rms_rows(x, g):
    ms = jnp.mean(x * x, axis=-1, keepdims=True)
    return x * lax.rsqrt(ms + EPS) * g


def _layer_vec(width, layer):
    return pl.BlockSpec((None, 1, width), lambda i: (layer, 0, 0))


def _rmsnorm_kernel(x_ref, g_ref, o_ref):
    o_ref[...] = _rms_rows(x_ref[...], g_ref[...]).astype(o_ref.dtype)


def _rmsnorm(x, g_stack, layer, *, tm=1024):
    m, d = x.shape
    return pl.pallas_call(
        _rmsnorm_kernel,
        out_shape=jax.ShapeDtypeStruct((m, d), BF16),
        grid=(m // tm,),
        in_specs=[pl.BlockSpec((tm, d), lambda i: (i, 0)), _layer_vec(d, layer)],
        out_specs=pl.BlockSpec((tm, d), lambda i: (i, 0)),
        compiler_params=_cparams(("arbitrary",)),
        name="rmsnorm",
    )(x, g_stack)


def _proj_nt_kernel(h_ref, wt_ref, o_ref):
    acc = lax.dot_general(h_ref[...], wt_ref[0].astype(BF16), NT_DIMS, preferred_element_type=F32)
    o_ref[...] = acc.astype(o_ref.dtype)


def _proj_nt(h, wt_stack, layer, row0, n, *, tm, tn, name):
    m, d = h.shape
    return pl.pallas_call(
        _proj_nt_kernel,
        out_shape=jax.ShapeDtypeStruct((m, n), BF16),
        grid=(m // tm, n // tn),
        in_specs=[pl.BlockSpec((tm, d), lambda i, j: (i, 0)),
                  pl.BlockSpec((pl.Element(1), pl.Element(tn), pl.Element(d)),
                               lambda i, j: (layer, pl.multiple_of(row0 + j * tn, SUBLANE), 0))],
        out_specs=pl.BlockSpec((tm, tn), lambda i, j: (i, j)),
        compiler_params=_cparams(("arbitrary", "arbitrary")),
        name=name,
    )(h, wt_stack)


def _proj_sigmoid_kernel(h_ref, w_ref, b_ref, o_ref):
    acc = jnp.dot(h_ref[...], w_ref[...].astype(BF16), preferred_element_type=F32)
    o_ref[...] = _sigmoid(acc + b_ref[...]).astype(o_ref.dtype)


def _proj_sigmoid(h, w_stack, b_stack, layer, *, tm, tn, name):
    m, d = h.shape
    n = w_stack.shape[2]
    return pl.pallas_call(
        _proj_sigmoid_kernel,
        out_shape=jax.ShapeDtypeStruct((m, n), BF16),
        grid=(m // tm, n // tn),
        in_specs=[pl.BlockSpec((tm, d), lambda i, j: (i, 0)),
                  pl.BlockSpec((None, d, tn), lambda i, j: (layer, 0, j)),
                  pl.BlockSpec((None, 1, tn), lambda i, j: (layer, 0, j))],
        out_specs=pl.BlockSpec((tm, tn), lambda i, j: (i, j)),
        compiler_params=_cparams(("arbitrary", "arbitrary")),
        name=name,
    )(h, w_stack, b_stack)


CONV_HALO = 32
CONV_ROWS = 64


def _conv_kernel(z_ref, w_ref, cb_ref, lg_ref, lb_ref, o_ref, abuf, sh, *, ts):
    t = pl.program_id(1)

    @pl.when(t == 0)
    def _():
        abuf[0:CONV_HALO, :] = jnp.zeros((CONV_HALO, CONV_CH), F32)

    @pl.when(t > 0)
    def _():
        abuf[0:CONV_HALO, :] = abuf[ts:ts + CONV_HALO, :]

    z = z_ref[...].astype(F32)
    abuf[CONV_HALO:CONV_HALO + ts, :] = z[:, :CONV_CH] * _sigmoid(z[:, CONV_CH:])

    sh_rows = sh.shape[1]
    for b in range(1, SUBLANE):
        sh[b] = abuf[b:b + sh_rows, :]

    first = CONV_HALO - (CONV_WIDTH - 1)
    for r0 in range(0, ts, CONV_ROWS):
        acc = abuf[r0 + CONV_HALO:r0 + CONV_HALO + CONV_ROWS, :] * w_ref[CONV_WIDTH - 1:CONV_WIDTH, :]
        for j in range(CONV_WIDTH - 1):
            res = (first + j) % SUBLANE
            base = r0 + first + j - res
            rows = abuf[base:base + CONV_ROWS, :] if res == 0 else sh[res, base:base + CONV_ROWS, :]
            acc = acc + rows * w_ref[j:j + 1, :]
        acc = acc + cb_ref[...]
        mu = jnp.mean(acc, axis=-1, keepdims=True)
        xc = acc - mu
        var = jnp.mean(xc * xc, axis=-1, keepdims=True)
        y = xc * lax.rsqrt(var + EPS) * lg_ref[...] + lb_ref[...]
        o_ref[r0:r0 + CONV_ROWS, :] = _silu(y).astype(o_ref.dtype)


def _conformer_conv(z3, conv_w, conv_b, ln_g, ln_b, layer, *, ts=512):
    b_sz, s, _ = z3.shape

    def par(a):
        return pl.BlockSpec((None,) + a.shape[1:], lambda b, t: (layer, 0, 0))

    return pl.pallas_call(
        functools.partial(_conv_kernel, ts=ts),
        out_shape=jax.ShapeDtypeStruct((b_sz, s, CONV_CH), BF16),
        grid=(b_sz, s // ts),
        in_specs=[
            pl.BlockSpec((None, ts, 2 * CONV_CH), lambda b, t: (b, t, OFF_GLU // (2 * CONV_CH))),
            par(conv_w), par(conv_b), par(ln_g), par(ln_b),
        ],
        out_specs=pl.BlockSpec((None, ts, CONV_CH), lambda b, t: (b, t, 0)),
        scratch_shapes=[pltpu.VMEM((CONV_HALO + ts, CONV_CH), F32),
                        pltpu.VMEM((SUBLANE, CONV_HALO + ts - SUBLANE, CONV_CH), F32)],
        compiler_params=_cparams(("arbitrary", "arbitrary")),
        name="conformer_conv",
    )(z3, conv_w, conv_b, ln_g, ln_b)


LOG2_E = 1.4426950408889634
GLA_QK = GLA_HEADS * GLA_DK
GLA_V = GLA_HEADS * GLA_DV


GLA_CPS = 4


def _gla_kernel(*refs, n_narrow):
    q_ref, k_ref, v_ref, r_ref, al_ref, wa_ref, ba_ref, ng_ref = refs[:8]
    w_f32 = refs[8:8 + n_narrow]
    o_ref = refs[8 + n_narrow]
    w_bf16 = refs[9 + n_narrow:9 + 2 * n_narrow]
    st, qf, kf, vf, bcs, pbig, intra = refs[9 + 2 * n_narrow:]

    for src, dst in zip(w_f32, w_bf16):
        dst[...] = src[...].astype(dst.dtype)

    @pl.when(pl.program_id(1) == 0)
    def _():
        st[...] = jnp.zeros(st.shape, F32)

    for ci in range(GLA_CPS):
        rs = slice(ci * GLA_CHUNK, (ci + 1) * GLA_CHUNK)
        _gla_chunk(q_ref.at[rs, :], k_ref.at[rs, :], v_ref.at[rs, :], r_ref.at[rs, :], al_ref.at[rs, :],
                   wa_ref, ba_ref, ng_ref, o_ref.at[rs, :],
                   st, qf.at[ci], kf.at[ci], vf.at[ci], bcs.at[ci], pbig.at[ci], intra.at[ci])


def _gla_chunk(q_ref, k_ref, v_ref, r_ref, al_ref, wa_ref, ba_ref, ng_ref, o_ref,
               st, qf, kf, vf, bcs, pbig, intra):
    c = GLA_CHUNK

    xa = jnp.dot(al_ref[...], wa_ref[...], preferred_element_type=F32) + ba_ref[...]
    la = (jnp.minimum(xa, 0.0) - jnp.log(1.0 + jnp.exp(-jnp.abs(xa)))) * (1.0 / GLA_TAU)
    row = lax.broadcasted_iota(jnp.int32, (c, c), 0)
    col = lax.broadcasted_iota(jnp.int32, (c, c), 1)
    tri = jnp.where(row >= col, 1.0, 0.0).astype(BF16)
    la_hi = la.astype(BF16)
    la_lo = (la - la_hi.astype(F32)).astype(BF16)
    bc = (jnp.dot(tri, la_hi, preferred_element_type=F32)
          + jnp.dot(tri, la_lo, preferred_element_type=F32))

    q = q_ref[...].astype(F32) * (GLA_DK ** -0.5)
    k = k_ref[...].astype(F32)
    qf[...] = q
    kf[...] = k
    vf[...] = v_ref[...].astype(F32)
    bcs[...] = bc * LOG2_E

    b_last = bc[c - 1:c, :]
    q_in = (q * jnp.exp(bc)).astype(BF16)
    k_dec = (k * jnp.exp(b_last - bc)).astype(BF16)
    e_last = jnp.exp(b_last)
    v_bf = v_ref[...]

    srow = lax.broadcasted_iota(jnp.int32, (GLA_QK, GLA_V), 0) // GLA_DK
    scol = lax.broadcasted_iota(jnp.int32, (GLA_QK, GLA_V), 1) // GLA_DV
    sel = jnp.where(srow == scol, 1.0, 0.0).astype(BF16)

    for j in range(c // GLA_SUB):
        t0 = j * GLA_SUB
        tj = c - t0
        q_rows = qf[t0:c, :]
        bc_rows = bcs[t0:c, :]
        t_loc = lax.broadcasted_iota(jnp.int32, (GLA_SUB, GLA_QK), 0)
        for sl in range(GLA_SUB):
            s = t0 + sl
            d = bc_rows - bcs[s:s + 1, :]
            d_own = jnp.minimum(d[:GLA_SUB, :], 0.0)
            d = d_own if tj == GLA_SUB else jnp.concatenate([d_own, d[GLA_SUB:, :]], axis=0)
            p = (q_rows * kf[s:s + 1, :]) * jnp.exp2(d)
            p_own = jnp.where(t_loc >= sl, p[:GLA_SUB, :], 0.0)
            p = p_own if tj == GLA_SUB else jnp.concatenate([p_own, p[GLA_SUB:, :]], axis=0)
            pbig[sl * tj:(sl + 1) * tj, :] = p.astype(BF16)
        r_all = jnp.dot(pbig[0:GLA_SUB * tj, :], sel, preferred_element_type=F32)
        acc = jnp.zeros((tj, GLA_V), F32)
        for sl in range(GLA_SUB):
            s = t0 + sl
            acc = acc + r_all[sl * tj:(sl + 1) * tj, :] * vf[s:s + 1, :]
        if j == 0:
            intra[...] = acc
        else:
            intra[t0:c, :] = intra[t0:c, :] + acc

    for h in range(GLA_HEADS):
        ks = slice(h * GLA_DK, (h + 1) * GLA_DK)
        vs = slice(h * GLA_DV, (h + 1) * GLA_DV)
        st_h = st[h]
        inter = lax.dot_general(q_in[:, ks], st_h.astype(BF16), (((1,), (1,)), ((), ())),
                                preferred_element_type=F32)
        upd = lax.dot_general(v_bf[:, vs], k_dec[:, ks], (((0,), (0,)), ((), ())),
                              preferred_element_type=F32)
        st[h] = st_h * e_last[:, ks] + upd
        o_h = inter + intra[:, vs]
        y = _rms_rows(o_h, ng_ref[...])
        o_ref[:, vs] = (y * _silu(r_ref[:, vs].astype(F32))).astype(o_ref.dtype)


def _gla(z3, zg3, w_a2p, b_a, norm_g, narrow, layer):
    b_sz, s, _ = z3.shape
    c = GLA_CHUNK
    rows = c * GLA_CPS
    cps = GLA_CPS
    t_steps = s // rows
    n_steps = b_sz * t_steps

    def zspec(width, off):
        return pl.BlockSpec((None, rows, width), lambda b, t: (b, t, off // width))

    slab = [w.shape[1] // n_steps for w in narrow]
    return pl.pallas_call(
        functools.partial(_gla_kernel, n_narrow=len(narrow)),
        out_shape=(jax.ShapeDtypeStruct((b_sz, s, GLA_V), BF16),
                   *[jax.ShapeDtypeStruct(w.shape[1:], BF16) for w in narrow]),
        grid=(b_sz, t_steps),
        in_specs=[
            zspec(GLA_QK, OFF_GQ), zspec(GLA_QK, OFF_GK), zspec(GLA_V, OFF_GV), zspec(GLA_V, OFF_GR),
            zspec(GATE_PAD, 0),
            pl.BlockSpec((GATE_PAD, GLA_QK), lambda b, t: (0, 0)),
            pl.BlockSpec((1, GLA_QK), lambda b, t: (0, 0)),
            pl.BlockSpec((1, GLA_DV), lambda b, t: (0, 0)),
            *[pl.BlockSpec((None, r, w.shape[2]), lambda b, t: (layer, b * t_steps + t, 0))
              for r, w in zip(slab, narrow)],
        ],
        out_specs=(pl.BlockSpec((None, rows, GLA_V), lambda b, t: (b, t, 0)),
                   *[pl.BlockSpec((r, w.shape[2]), lambda b, t: (b * t_steps + t, 0))
                     for r, w in zip(slab, narrow)]),
        scratch_shapes=[
            pltpu.VMEM((GLA_HEADS, GLA_DV, GLA_DK), F32),
            pltpu.VMEM((cps, c, GLA_QK), F32),
            pltpu.VMEM((cps, c, GLA_QK), F32),
            pltpu.VMEM((cps, c, GLA_V), F32),
            pltpu.VMEM((cps, c, GLA_QK), F32),
            pltpu.VMEM((cps, GLA_SUB * c, GLA_QK), BF16),
            pltpu.VMEM((cps, c, GLA_V), F32),
        ],
        compiler_params=_cparams(("arbitrary", "arbitrary")),
        name="gla",
    )(z3, z3, z3, z3, zg3, w_a2p, b_a, norm_g, *narrow)


NEG_INF = float("-inf")
MOBA_GROUP = 4


def _fold_rows(x, op):
    return op(x.reshape(x.shape[0] // SUBLANE, SUBLANE, x.shape[1]), axis=0)


MOBA_HPS = 4


def _moba_kernel(*refs, n_blk):
    hps = MOBA_HPS
    q_refs, k_refs, v_refs = refs[:hps], refs[hps:2 * hps], refs[2 * hps:3 * hps]
    gq_ref, gk_ref, o_ref, kn, vt, kmean, chosen, s_scr, s_own, m_acc, l_acc, o_acc = refs[3 * hps:]
    qi = pl.program_id(2)
    blk = MOBA_BLOCK
    grp = MOBA_GROUP
    heads = range(hps)

    @pl.when(qi == 0)
    def _():
        for hh in heads:
            for n in range(n_blk):
                kb = _rms_rows(k_refs[hh][n * blk:(n + 1) * blk, :].astype(F32), gk_ref[...])
                kn[hh, n * blk:(n + 1) * blk, :] = kb.astype(BF16)
                kmean[hh, n:n + 1, :] = jnp.mean(kb, axis=0, keepdims=True)
                vt[hh, n] = v_refs[hh][n * blk:(n + 1) * blk, :].astype(F32).T.astype(BF16)

    brow = lax.broadcasted_iota(jnp.int32, (n_blk, blk), 0)
    past = brow < qi
    k_i = lax.broadcasted_iota(jnp.int32, (blk, blk), 0)
    q_i = lax.broadcasted_iota(jnp.int32, (blk, blk), 1)
    own = pl.multiple_of(qi * blk, blk)
    q_bf = []
    for hh in heads:
        qn = _rms_rows(q_refs[hh][...].astype(F32), gq_ref[...])
        q_bf.append((qn * (MOBA_HD ** -0.5 * LOG2_E)).astype(BF16))
        sb = lax.dot_general(kmean[hh], qn, NT_DIMS, precision=lax.Precision.HIGHEST,
                             preferred_element_type=F32)
        sb = jnp.where(past, sb, NEG_INF)
        rank = jnp.zeros((n_blk, blk), F32)
        for m in range(n_blk):
            cm = sb[m:m + 1, :]
            ge = jnp.where(cm >= sb, 1.0, 0.0)
            gt = jnp.where(cm > sb, 1.0, 0.0)
            rank = rank + jnp.where(brow > m, ge, gt)
        chosen[hh] = jnp.where(past, jnp.where(rank < float(MOBA_TOPK), 1.0, 0.0), 0.0)
        so = lax.dot_general(kn[hh, pl.ds(own, blk), :], q_bf[hh], NT_DIMS, preferred_element_type=F32)
        so = jnp.where(k_i <= q_i, so, NEG_INF)
        s_own[hh] = so
        m_acc[hh] = _fold_rows(so, jnp.max)

    def for_past_blocks(visit):
        n_full = qi // grp

        def body(g, c):
            visit(g * grp, grp)
            return c

        lax.fori_loop(0, n_full, body, 0)
        for rem in range(1, grp):
            @pl.when(qi % grp == rem)
            def _():
                visit(n_full * grp, rem)

    def scores(first, count):
        start = pl.multiple_of(first * blk, blk)
        for hh in heads:
            sg = lax.dot_general(kn[hh, pl.ds(start, count * blk), :], q_bf[hh], NT_DIMS,
                                 preferred_element_type=F32)
            m_run = m_acc[hh]
            for b in range(count):
                sn = jnp.where(chosen[hh, pl.ds(first + b, 1), :] > 0.0, sg[b * blk:(b + 1) * blk, :], NEG_INF)
                s_scr[hh, first + b] = sn
                m_run = jnp.maximum(m_run, _fold_rows(sn, jnp.max))
            m_acc[hh] = m_run

    for_past_blocks(scores)
    m_rows = [jnp.max(m_acc[hh], axis=0, keepdims=True) for hh in heads]

    for hh in heads:
        p_own = jnp.exp2(s_own[hh] - m_rows[hh])
        l_acc[hh] = _fold_rows(p_own, jnp.sum)
        o_acc[hh] = jnp.dot(vt[hh, qi], p_own.astype(BF16), preferred_element_type=F32)

    def values(first, count):
        for hh in heads:
            l_run = l_acc[hh]
            ps, vs = [], []
            for b in range(count):
                pn = jnp.exp2(s_scr[hh, first + b] - m_rows[hh])
                l_run = l_run + _fold_rows(pn, jnp.sum)
                ps.append(pn.astype(BF16))
                vs.append(vt[hh, first + b])
            l_acc[hh] = l_run
            o_acc[hh] += jnp.dot(jnp.concatenate(vs, axis=1), jnp.concatenate(ps, axis=0),
                                 preferred_element_type=F32)

    for_past_blocks(values)
    for hh in heads:
        l_row = jnp.sum(l_acc[hh], axis=0, keepdims=True)
        o_ref[:, hh * MOBA_HD:(hh + 1) * MOBA_HD] = (o_acc[hh] / l_row).T.astype(o_ref.dtype)


def _moba(z3, gq, gk):
    b_sz, s, _ = z3.shape
    n_blk = s // MOBA_BLOCK
    hd = MOBA_HD
    hps = MOBA_HPS

    def head_spec(rows, off, hh, row_index):
        return pl.BlockSpec((None, rows, hd), lambda b, h, i: (b, row_index(i), off // hd + h * hps + hh))

    q_specs = [head_spec(MOBA_BLOCK, OFF_MQ, hh, lambda i: i) for hh in range(hps)]
    k_specs = [head_spec(s, OFF_MK, hh, lambda i: 0) for hh in range(hps)]
    v_specs = [head_spec(s, OFF_MV, hh, lambda i: 0) for hh in range(hps)]
    vec = pl.BlockSpec((1, hd), lambda b, h, i: (0, 0))
    return pl.pallas_call(
        functools.partial(_moba_kernel, n_blk=n_blk),
        out_shape=jax.ShapeDtypeStruct((b_sz, s, MOBA_HEADS * hd), BF16),
        grid=(b_sz, MOBA_HEADS // hps, n_blk),
        in_specs=q_specs + k_specs + v_specs + [vec, vec],
        out_specs=pl.BlockSpec((None, MOBA_BLOCK, hps * hd), lambda b, h, i: (b, i, h)),
        scratch_shapes=[
            pltpu.VMEM((hps, s, hd), BF16),
            pltpu.VMEM((hps, n_blk, hd, MOBA_BLOCK), BF16),
            pltpu.VMEM((hps, n_blk, hd), F32),
            pltpu.VMEM((hps, n_blk, MOBA_BLOCK), F32),
            pltpu.VMEM((hps, n_blk, MOBA_BLOCK, MOBA_BLOCK), F32),
            pltpu.VMEM((hps, MOBA_BLOCK, MOBA_BLOCK), F32),
            pltpu.VMEM((hps, SUBLANE, MOBA_BLOCK), F32),
            pltpu.VMEM((hps, SUBLANE, MOBA_BLOCK), F32),
            pltpu.VMEM((hps, hd, MOBA_BLOCK), F32),
        ],
        compiler_params=_cparams(("arbitrary", "arbitrary", "arbitrary")),
        name="moba",
    )(*([z3] * (3 * hps)), gq, gk)


MERGE_COLS = 512


def _merge_kernel(a_ref, gl_ref, mo_ref, gt_ref, x_ref, wc_ref, wg_ref, wm_ref, wo_ref, gn_ref,
                  wup_ref, wdn_ref, o_ref, hn_ref, wup_o, wdn_o, mg):
    d = D_MODEL
    wup_o[...] = wup_ref[...].astype(wup_o.dtype)
    wdn_o[...] = wdn_ref[...].astype(wdn_o.dtype)
    for c0 in range(0, d, MERGE_COLS):
        cs = slice(c0, c0 + MERGE_COLS)
        ya = jnp.dot(a_ref[...], wc_ref[:, cs], preferred_element_type=F32)
        yb = jnp.dot(gl_ref[...], wg_ref[:, cs], preferred_element_type=F32)
        yc = jnp.dot(mo_ref[...], wm_ref[:, cs], preferred_element_type=F32)
        g0 = gt_ref[:, c0:c0 + MERGE_COLS].astype(F32)
        g1 = gt_ref[:, d + c0:d + c0 + MERGE_COLS].astype(F32)
        g2 = gt_ref[:, 2 * d + c0:2 * d + c0 + MERGE_COLS].astype(F32)
        mg[:, cs] = (g0 * ya + g1 * yb + g2 * yc).astype(BF16)
    x_new = x_ref[...] + jnp.dot(mg[...], wo_ref[...], preferred_element_type=F32)
    o_ref[...] = x_new
    hn_ref[...] = _rms_rows(x_new, gn_ref[...]).astype(hn_ref.dtype)


def _resident(w, layer):
    if w.ndim == 2:
        return pl.BlockSpec(w.shape, lambda i: (0, 0), pipeline_mode=pl.Buffered(1))
    return pl.BlockSpec((None,) + w.shape[1:], lambda i: (layer, 0, 0), pipeline_mode=pl.Buffered(1))


def _merge(a_act, o_gla, o_moba, gates, x, wc, wg, wm, wo, g_next, w_up, w_down, layer, *, tm=256):
    m, d = x.shape
    steps = m // tm
    up_rows, dn_rows = w_up.shape[1] // steps, w_down.shape[1] // steps

    def rows(width):
        return pl.BlockSpec((tm, width), lambda i: (i, 0))

    return pl.pallas_call(
        _merge_kernel,
        out_shape=(jax.ShapeDtypeStruct((m, d), F32), jax.ShapeDtypeStruct((m, d), BF16),
                   jax.ShapeDtypeStruct(w_up.shape[1:], BF16), jax.ShapeDtypeStruct(w_down.shape[1:], BF16)),
        grid=(steps,),
        in_specs=[rows(a_act.shape[1]), rows(o_gla.shape[1]), rows(o_moba.shape[1]), rows(gates.shape[1]),
                  rows(d), _resident(wc, layer), _resident(wg, layer), _resident(wm, layer),
                  _resident(wo, layer), _layer_vec(d, layer),
                  pl.BlockSpec((None, up_rows, w_up.shape[2]), lambda i: (layer, i, 0)),
                  pl.BlockSpec((None, dn_rows, w_down.shape[2]), lambda i: (layer, i, 0))],
        out_specs=(rows(d), rows(d),
                   pl.BlockSpec((up_rows, w_up.shape[2]), lambda i: (i, 0)),
                   pl.BlockSpec((dn_rows, w_down.shape[2]), lambda i: (i, 0))),
        scratch_shapes=[pltpu.VMEM((tm, d), BF16)],
        compiler_params=_cparams(("arbitrary",)),
        name="merge",
    )(a_act, o_gla, o_moba, gates, x, wc, wg, wm, wo, g_next, w_up, w_down)


def _shift_rows(u, k, carry):
    rolled = pltpu.roll(u, k, 0)
    head = jnp.where(lax.broadcasted_iota(jnp.int32, carry.shape, 0) < k,
                     pltpu.roll(carry, k, 0), rolled[:SUBLANE, :])
    return jnp.concatenate([head, rolled[SUBLANE:, :]], axis=0)


def _ffn_kernel(h_ref, wug_ref, wuv_ref, cwg_ref, cwv_ref, wd_ref, o_ref, cg, cv, *, tiles_per_seq):
    i = pl.program_id(0)
    j = pl.program_id(1)
    tm = h_ref.shape[0]

    @pl.when(j == 0)
    def _():
        o_ref[...] = jnp.zeros(o_ref.shape, o_ref.dtype)

    @pl.when(i % tiles_per_seq == 0)
    def _():
        cg[j] = jnp.zeros(cg.shape[1:], F32)
        cv[j] = jnp.zeros(cv.shape[1:], F32)

    h = h_ref[...]

    def conv(w_ref, cw_ref, carry_ref):
        u = jnp.dot(h, w_ref[...], preferred_element_type=F32)
        carry = carry_ref[j]
        out = (u * cw_ref[2:3, :] + _shift_rows(u, 1, carry) * cw_ref[1:2, :]
               + _shift_rows(u, 2, carry) * cw_ref[0:1, :])
        carry_ref[j] = u[tm - SUBLANE:, :]
        return out

    ug = conv(wug_ref, cwg_ref, cg)
    uv = conv(wuv_ref, cwv_ref, cv)
    act = (_silu(ug) * uv).astype(BF16)
    o_ref[...] += jnp.dot(act, wd_ref[...], preferred_element_type=F32)


def _ffn(h, w_up, conv_w, w_down, layer, *, seq, tm=1024, tf=512):
    m, d = h.shape
    f = w_down.shape[0]
    nf = f // tf
    return pl.pallas_call(
        functools.partial(_ffn_kernel, tiles_per_seq=seq // tm),
        out_shape=jax.ShapeDtypeStruct((m, d), F32),
        grid=(m // tm, nf),
        in_specs=[
            pl.BlockSpec((tm, d), lambda i, j: (i, 0)),
            pl.BlockSpec((d, tf), lambda i, j: (0, j)),
            pl.BlockSpec((d, tf), lambda i, j: (0, j + nf)),
            pl.BlockSpec((None, FFN_CONV_WIDTH, tf), lambda i, j: (layer, 0, j)),
            pl.BlockSpec((None, FFN_CONV_WIDTH, tf), lambda i, j: (layer, 0, j + nf)),
            pl.BlockSpec((tf, d), lambda i, j: (j, 0)),
        ],
        out_specs=pl.BlockSpec((tm, d), lambda i, j: (i, 0)),
        scratch_shapes=[
            pltpu.VMEM((nf, SUBLANE, tf), F32),
            pltpu.VMEM((nf, SUBLANE, tf), F32),
        ],
        compiler_params=_cparams(("arbitrary", "arbitrary")),
        name="conv_ffn",
    )(h, w_up, w_up, conv_w, conv_w, w_down)


PLE_COLS = 512


def _ple_kernel(x_ref, y_ref, p_ref, g_ref, wg_ref, wp_ref, *rest):
    h_scr = rest[-1]
    o_ref = rest[0] if len(rest) == 2 else rest[1]
    x2 = x_ref[...] + y_ref[...]
    o_ref[...] = x2
    h_scr[...] = _rms_rows(x2, g_ref[...]).astype(h_scr.dtype)
    p_bf = p_ref[...].astype(BF16)
    for c0 in range(0, D_MODEL, PLE_COLS):
        cs = slice(c0, c0 + PLE_COLS)
        gate = _sigmoid(jnp.dot(h_scr[...], wg_ref[:, cs], preferred_element_type=F32))
        emb = jnp.dot(p_bf, wp_ref[:, cs], preferred_element_type=F32)
        o_ref[:, cs] = o_ref[:, cs] + gate * emb
    if len(rest) == 4:
        gn_ref, _, hn_ref, _ = rest
        hn_ref[...] = _rms_rows(o_ref[...], gn_ref[...]).astype(hn_ref.dtype)


def _ple(x, y, p, g_ple, w_gate, w_ple, layer, g_next, *, tm=512):
    m, d = x.shape
    rows = lambda width: pl.BlockSpec((tm, width), lambda i: (i, 0))
    in_specs = [rows(d), rows(d), pl.BlockSpec((None, tm, PLE_DIM), lambda i: (layer, i, 0)),
                _layer_vec(d, layer), _resident(w_gate, layer), _resident(w_ple, layer)]
    args = [x, y, p, g_ple, w_gate, w_ple]
    out_shape = [jax.ShapeDtypeStruct((m, d), F32)]
    out_specs = [rows(d)]
    if g_next is not None:
        in_specs.append(_layer_vec(d, layer + 1))
        args.append(g_next)
        out_shape.append(jax.ShapeDtypeStruct((m, d), BF16))
        out_specs.append(rows(d))
    return pl.pallas_call(
        _ple_kernel,
        out_shape=tuple(out_shape),
        grid=(m // tm,),
        in_specs=in_specs,
        out_specs=tuple(out_specs),
        scratch_shapes=[pltpu.VMEM((tm, d), BF16)],
        compiler_params=_cparams(("arbitrary",)),
        name="ple",
    )(*args)


def kernel(x, p, norm_mix_g, w_in, conv_w, conv_b, conv_ln_g, conv_ln_b, w_conv_out, gla_w_a2, gla_b_a,
           gla_norm_g, w_gla_out, moba_q_norm_g, moba_k_norm_g, w_moba_out, w_gate, b_gate, w_out,
           norm_ffn_g, w_up, ffn_conv_w, w_down, norm_ple_g, w_ple_gate, w_ple):
    b_sz, s, d = x.shape
    depth = w_in.shape[0]
    m = b_sz * s
    xf = x.reshape(m, d)
    p2 = p.reshape(depth, m, PLE_DIM)
    row = lambda v: v.reshape(1, -1)
    stack_vec = lambda v: v.reshape(depth, 1, -1)
    w_in_t = jnp.swapaxes(w_in, 1, 2)
    w_ple_b = w_ple.astype(BF16)
    w_a2p = jnp.concatenate(
        [gla_w_a2, jnp.zeros((depth, GATE_PAD - GLA_GATE_RANK, GLA_QK), F32)], axis=1).astype(BF16)
    g_mix, g_ffn, g_ple, b_gate3 = (stack_vec(v) for v in (norm_mix_g, norm_ffn_g, norm_ple_g, b_gate))
    conv_b3, conv_ln_g3, conv_ln_b3 = (stack_vec(v) for v in (conv_b, conv_ln_g, conv_ln_b))

    h = _rmsnorm(xf, g_mix, 0)
    for i in range(depth):
        z3 = _proj_nt(h, w_in_t, i, 0, W_CONV_GLA, tm=2048, tn=512,
                      name="in_proj_conv_gla").reshape(b_sz, s, W_CONV_GLA)
        zg3 = _proj_nt(h, w_in_t, i, W_CONV_GLA, GATE_PAD, tm=2048, tn=GATE_PAD,
                       name="in_proj_gate").reshape(b_sz, s, GATE_PAD)
        zm3 = _proj_nt(h, w_in_t, i, W_CONV_GLA + GLA_GATE_RANK, W_MOBA, tm=2048, tn=1024,
                       name="in_proj_moba").reshape(b_sz, s, W_MOBA)
        gates = _proj_sigmoid(h, w_gate, b_gate3, i, tm=2048, tn=1024, name="gate_proj")
        a_act = _conformer_conv(z3, conv_w, conv_b3, conv_ln_g3, conv_ln_b3, i)
        o_gla, wc_b, wg_b, wm_b, wo_b, w_pg_b = _gla(
            z3, zg3, w_a2p[i], row(gla_b_a[i]), row(gla_norm_g[i]),
            (w_conv_out, w_gla_out, w_moba_out, w_out, w_ple_gate), i)
        o_moba = _moba(zm3, row(moba_q_norm_g[i]), row(moba_k_norm_g[i]))
        xf, h, w_up_b, w_down_b = _merge(a_act.reshape(m, -1), o_gla.reshape(m, -1), o_moba.reshape(m, -1),
                                         gates, xf, wc_b, wg_b, wm_b, wo_b, g_ffn, w_up, w_down, i)
        y = _ffn(h, w_up_b, ffn_conv_w, w_down_b, i, seq=s)
        if i + 1 < depth:
            xf, h = _ple(xf, y, p2, g_ple, w_pg_b, w_ple_b, i, g_mix)
        else:
            (xf,) = _ple(xf, y, p2, g_ple, w_pg_b, w_ple_b, i, None)
    return xf.reshape(b_sz, s, d)
```

```python
import functools

import jax
import jax.numpy as jnp
from jax import lax
from jax.experimental import pallas as pl
from jax.experimental.pallas import tpu as pltpu

F32 = jnp.float32
BF16 = jnp.bfloat16

D_MODEL = 2048
PLE_DIM = 256
EPS = 1e-6
CONV_CH = 512
CONV_WIDTH = 31
GLA_HEADS = 4
GLA_DV = 128
GLA_DK = 64
GLA_GATE_RANK = 16
GLA_TAU = 16.0
GLA_CHUNK = 64
GLA_SUB = 16
MOBA_HEADS = 8
MOBA_HD = 128
MOBA_BLOCK = 256
MOBA_TOPK = 3
FFN_DIM = 5632
FFN_CONV_WIDTH = 3

LANE = 128
SUBLANE = 8
GATE_PAD = LANE
OFF_GLU = 0
OFF_GQ = OFF_GLU + 2 * CONV_CH
OFF_GK = OFF_GQ + GLA_HEADS * GLA_DK
OFF_GV = OFF_GK + GLA_HEADS * GLA_DK
OFF_GR = OFF_GV + GLA_HEADS * GLA_DV
W_CONV_GLA = OFF_GR + GLA_HEADS * GLA_DV
OFF_MQ = 0
OFF_MK = OFF_MQ + MOBA_HEADS * MOBA_HD
OFF_MV = OFF_MK + MOBA_HEADS * MOBA_HD
W_MOBA = OFF_MV + MOBA_HEADS * MOBA_HD
NT_DIMS = (((1,), (1,)), ((), ()))

VMEM_LIMIT = 56 * 1024 * 1024


def _cparams(sem):
    return pltpu.CompilerParams(dimension_semantics=sem, vmem_limit_bytes=VMEM_LIMIT)


def _sigmoid(x):
    return 1.0 / (1.0 + jnp.exp(-x))


def _silu(x):
    return x * _sigmoid(x)


def _rms_rows(x, g):
    ms = jnp.mean(x * x, axis=-1, keepdims=True)
    return x * lax.rsqrt(ms + EPS) * g


def _layer_vec(width, layer):
    return pl.BlockSpec((None, 1, width), lambda i: (layer, 0, 0))


def _rmsnorm_kernel(x_ref, g_ref, o_ref):
    o_ref[...] = _rms_rows(x_ref[...], g_ref[...]).astype(o_ref.dtype)


def _rmsnorm(x, g_stack, layer, *, tm=1024):
    m, d = x.shape
    return pl.pallas_call(
        _rmsnorm_kernel,
        out_shape=jax.ShapeDtypeStruct((m, d), BF16),
        grid=(m // tm,),
        in_specs=[pl.BlockSpec((tm, d), lambda i: (i, 0)), _layer_vec(d, layer)],
        out_specs=pl.BlockSpec((tm, d), lambda i: (i, 0)),
        compiler_params=_cparams(("arbitrary",)),
        name="rmsnorm",
    )(x, g_stack)


def _proj_nt_kernel(h_ref, wt_ref, o_ref):
    acc = lax.dot_general(h_ref[...], wt_ref[0].astype(BF16), NT_DIMS, preferred_element_type=F32)
    o_ref[...] = acc.astype(o_ref.dtype)


def _proj_nt(h, wt_stack, layer, row0, n, *, tm, tn, name):
    m, d = h.shape
    return pl.pallas_call(
        _proj_nt_kernel,
        out_shape=jax.ShapeDtypeStruct((m, n), BF16),
        grid=(m // tm, n // tn),
        in_specs=[pl.BlockSpec((tm, d), lambda i, j: (i, 0)),
                  pl.BlockSpec((pl.Element(1), pl.Element(tn), pl.Element(d)),
                               lambda i, j: (layer, pl.multiple_of(row0 + j * tn, SUBLANE), 0))],
        out_specs=pl.BlockSpec((tm, tn), lambda i, j: (i, j)),
        compiler_params=_cparams(("arbitrary", "arbitrary")),
        name=name,
    )(h, wt_stack)


def _proj_sigmoid_kernel(h_ref, w_ref, b_ref, o_ref):
    acc = jnp.dot(h_ref[...], w_ref[...].astype(BF16), preferred_element_type=F32)
    o_ref[...] = _sigmoid(acc + b_ref[...]).astype(o_ref.dtype)


def _proj_sigmoid(h, w_stack, b_stack, layer, *, tm, tn, name):
    m, d = h.shape
    n = w_stack.shape[2]
    return pl.pallas_call(
        _proj_sigmoid_kernel,
        out_shape=jax.ShapeDtypeStruct((m, n), BF16),
        grid=(m // tm, n // tn),
        in_specs=[pl.BlockSpec((tm, d), lambda i, j: (i, 0)),
                  pl.BlockSpec((None, d, tn), lambda i, j: (layer, 0, j)),
                  pl.BlockSpec((None, 1, tn), lambda i, j: (layer, 0, j))],
        out_specs=pl.BlockSpec((tm, tn), lambda i, j: (i, j)),
        compiler_params=_cparams(("arbitrary", "arbitrary")),
        name=name,
    )(h, w_stack, b_stack)


CONV_HALO = 32
CONV_ROWS = 64


def _conv_kernel(z_ref, w_ref, cb_ref, lg_ref, lb_ref, o_ref, abuf, sh, *, ts):
    t = pl.program_id(1)

    @pl.when(t == 0)
    def _():
        abuf[0:CONV_HALO, :] = jnp.zeros((CONV_HALO, CONV_CH), F32)

    @pl.when(t > 0)
    def _():
        abuf[0:CONV_HALO, :] = abuf[ts:ts + CONV_HALO, :]

    z = z_ref[...].astype(F32)
    abuf[CONV_HALO:CONV_HALO + ts, :] = z[:, :CONV_CH] * _sigmoid(z[:, CONV_CH:])

    sh_rows = sh.shape[1]
    for b in range(1, SUBLANE):
        sh[b] = abuf[b:b + sh_rows, :]

    first = CONV_HALO - (CONV_WIDTH - 1)
    for r0 in range(0, ts, CONV_ROWS):
        acc = abuf[r0 + CONV_HALO:r0 + CONV_HALO + CONV_ROWS, :] * w_ref[CONV_WIDTH - 1:CONV_WIDTH, :]
        for j in range(CONV_WIDTH - 1):
            res = (first + j) % SUBLANE
            base = r0 + first + j - res
            rows = abuf[base:base + CONV_ROWS, :] if res == 0 else sh[res, base:base + CONV_ROWS, :]
            acc = acc + rows * w_ref[j:j + 1, :]
        acc = acc + cb_ref[...]
        mu = jnp.mean(acc, axis=-1, keepdims=True)
        xc = acc - mu
        var = jnp.mean(xc * xc, axis=-1, keepdims=True)
        y = xc * lax.rsqrt(var + EPS) * lg_ref[...] + lb_ref[...]
        o_ref[r0:r0 + CONV_ROWS, :] = _silu(y).astype(o_ref.dtype)


def _conformer_conv(z3, conv_w, conv_b, ln_g, ln_b, layer, *, ts=512):
    b_sz, s, _ = z3.shape

    def par(a):
        return pl.BlockSpec((None,) + a.shape[1:], lambda b, t: (layer, 0, 0))

    return pl.pallas_call(
        functools.partial(_conv_kernel, ts=ts),
        out_shape=jax.ShapeDtypeStruct((b_sz, s, CONV_CH), BF16),
        grid=(b_sz, s // ts),
        in_specs=[
            pl.BlockSpec((None, ts, 2 * CONV_CH), lambda b, t: (b, t, OFF_GLU // (2 * CONV_CH))),
            par(conv_w), par(conv_b), par(ln_g), par(ln_b),
        ],
        out_specs=pl.BlockSpec((None, ts, CONV_CH), lambda b, t: (b, t, 0)),
        scratch_shapes=[pltpu.VMEM((CONV_HALO + ts, CONV_CH), F32),
                        pltpu.VMEM((SUBLANE, CONV_HALO + ts - SUBLANE, CONV_CH), F32)],
        compiler_params=_cparams(("arbitrary", "arbitrary")),
        name="conformer_conv",
    )(z3, conv_w, conv_b, ln_g, ln_b)


LOG2_E = 1.4426950408889634
GLA_QK = GLA_HEADS * GLA_DK
GLA_V = GLA_HEADS * GLA_DV


GLA_CPS = 8


def _gla_kernel(*refs, n_narrow):
    q_ref, k_ref, v_ref, r_ref, al_ref, wa_ref, ba_ref, ng_ref = refs[:8]
    w_f32 = refs[8:8 + n_narrow]
    o_ref = refs[8 + n_narrow]
    w_bf16 = refs[9 + n_narrow:9 + 2 * n_narrow]
    st, qf, kf, vf, bcs, pbig, intra = refs[9 + 2 * n_narrow:]

    for src, dst in zip(w_f32, w_bf16):
        dst[...] = src[...].astype(dst.dtype)

    @pl.when(pl.program_id(1) == 0)
    def _():
        st[...] = jnp.zeros(st.shape, F32)

    for ci in range(GLA_CPS):
        rs = slice(ci * GLA_CHUNK, (ci + 1) * GLA_CHUNK)
        _gla_chunk(q_ref.at[rs, :], k_ref.at[rs, :], v_ref.at[rs, :], r_ref.at[rs, :], al_ref.at[rs, :],
                   wa_ref, ba_ref, ng_ref, o_ref.at[rs, :],
                   st, qf.at[ci], kf.at[ci], vf.at[ci], bcs.at[ci], pbig.at[ci], intra.at[ci])


def _gla_chunk(q_ref, k_ref, v_ref, r_ref, al_ref, wa_ref, ba_ref, ng_ref, o_ref,
               st, qf, kf, vf, bcs, pbig, intra):
    c = GLA_CHUNK

    xa = jnp.dot(al_ref[...], wa_ref[...], preferred_element_type=F32) + ba_ref[...]
    la = (jnp.minimum(xa, 0.0) - jnp.log(1.0 + jnp.exp(-jnp.abs(xa)))) * (1.0 / GLA_TAU)
    row = lax.broadcasted_iota(jnp.int32, (c, c), 0)
    col = lax.broadcasted_iota(jnp.int32, (c, c), 1)
    tri = jnp.where(row >= col, 1.0, 0.0).astype(BF16)
    la_hi = la.astype(BF16)
    la_lo = (la - la_hi.astype(F32)).astype(BF16)
    bc = (jnp.dot(tri, la_hi, preferred_element_type=F32)
          + jnp.dot(tri, la_lo, preferred_element_type=F32))

    q = q_ref[...].astype(F32) * (GLA_DK ** -0.5)
    k = k_ref[...].astype(F32)
    qf[...] = q
    kf[...] = k
    vf[...] = v_ref[...].astype(F32)
    bcs[...] = bc * LOG2_E

    b_last = bc[c - 1:c, :]
    q_in = (q * jnp.exp(bc)).astype(BF16)
    k_dec = (k * jnp.exp(b_last - bc)).astype(BF16)
    e_last = jnp.exp(b_last)
    v_bf = v_ref[...]

    srow = lax.broadcasted_iota(jnp.int32, (GLA_QK, GLA_V), 0) // GLA_DK
    scol = lax.broadcasted_iota(jnp.int32, (GLA_QK, GLA_V), 1) // GLA_DV
    sel = jnp.where(srow == scol, 1.0, 0.0).astype(BF16)

    for j in range(c // GLA_SUB):
        t0 = j * GLA_SUB
        tj = c - t0
        q_rows = qf[t0:c, :]
        bc_rows = bcs[t0:c, :]
        t_loc = lax.broadcasted_iota(jnp.int32, (GLA_SUB, GLA_QK), 0)
        for sl in range(GLA_SUB):
            s = t0 + sl
            d = bc_rows - bcs[s:s + 1, :]
            d_own = jnp.minimum(d[:GLA_SUB, :], 0.0)
            d = d_own if tj == GLA_SUB else jnp.concatenate([d_own, d[GLA_SUB:, :]], axis=0)
            p = (q_rows * kf[s:s + 1, :]) * jnp.exp2(d)
            p_own = jnp.where(t_loc >= sl, p[:GLA_SUB, :], 0.0)
            p = p_own if tj == GLA_SUB else jnp.concatenate([p_own, p[GLA_SUB:, :]], axis=0)
            pbig[sl * tj:(sl + 1) * tj, :] = p.astype(BF16)
        r_all = jnp.dot(pbig[0:GLA_SUB * tj, :], sel, preferred_element_type=F32)
        acc = jnp.zeros((tj, GLA_V), F32)
        for sl in range(GLA_SUB):
            s = t0 + sl
            acc = acc + r_all[sl * tj:(sl + 1) * tj, :] * vf[s:s + 1, :]
        if j == 0:
            intra[...] = acc
        else:
            intra[t0:c, :] = intra[t0:c, :] + acc

    for h in range(GLA_HEADS):
        ks = slice(h * GLA_DK, (h + 1) * GLA_DK)
        vs = slice(h * GLA_DV, (h + 1) * GLA_DV)
        st_h = st[h]
        inter = lax.dot_general(q_in[:, ks], st_h.astype(BF16), (((1,), (1,)), ((), ())),
                                preferred_element_type=F32)
        upd = lax.dot_general(v_bf[:, vs], k_dec[:, ks], (((0,), (0,)), ((), ())),
                              preferred_element_type=F32)
        st[h] = st_h * e_last[:, ks] + upd
        o_h = inter + intra[:, vs]
        y = _rms_rows(o_h, ng_ref[...])
        o_ref[:, vs] = (y * _silu(r_ref[:, vs].astype(F32))).astype(o_ref.dtype)


def _gla(z3, zg3, w_a2p, b_a, norm_g, narrow, layer):
    b_sz, s, _ = z3.shape
    c = GLA_CHUNK
    rows = c * GLA_CPS
    cps = GLA_CPS
    t_steps = s // rows
    n_steps = b_sz * t_steps

    def zspec(width, off):
        return pl.BlockSpec((None, rows, width), lambda b, t: (b, t, off // width))

    slab = [w.shape[1] // n_steps for w in narrow]
    return pl.pallas_call(
        functools.partial(_gla_kernel, n_narrow=len(narrow)),
        out_shape=(jax.ShapeDtypeStruct((b_sz, s, GLA_V), BF16),
                   *[jax.ShapeDtypeStruct(w.shape[1:], BF16) for w in narrow]),
        grid=(b_sz, t_steps),
        in_specs=[
            zspec(GLA_QK, OFF_GQ), zspec(GLA_QK, OFF_GK), zspec(GLA_V, OFF_GV), zspec(GLA_V, OFF_GR),
            zspec(GATE_PAD, 0),
            pl.BlockSpec((GATE_PAD, GLA_QK), lambda b, t: (0, 0)),
            pl.BlockSpec((1, GLA_QK), lambda b, t: (0, 0)),
            pl.BlockSpec((1, GLA_DV), lambda b, t: (0, 0)),
            *[pl.BlockSpec((None, r, w.shape[2]), lambda b, t: (layer, b * t_steps + t, 0))
              for r, w in zip(slab, narrow)],
        ],
        out_specs=(pl.BlockSpec((None, rows, GLA_V), lambda b, t: (b, t, 0)),
                   *[pl.BlockSpec((r, w.shape[2]), lambda b, t: (b * t_steps + t, 0))
                     for r, w in zip(slab, narrow)]),
        scratch_shapes=[
            pltpu.VMEM((GLA_HEADS, GLA_DV, GLA_DK), F32),
            pltpu.VMEM((cps, c, GLA_QK), F32),
            pltpu.VMEM((cps, c, GLA_QK), F32),
            pltpu.VMEM((cps, c, GLA_V), F32),
            pltpu.VMEM((cps, c, GLA_QK), F32),
            pltpu.VMEM((cps, GLA_SUB * c, GLA_QK), BF16),
            pltpu.VMEM((cps, c, GLA_V), F32),
        ],
        compiler_params=_cparams(("arbitrary", "arbitrary")),
        name="gla",
    )(z3, z3, z3, z3, zg3, w_a2p, b_a, norm_g, *narrow)


NEG_INF = float("-inf")
MOBA_GROUP = 4


def _fold_rows(x, op):
    return op(x.reshape(x.shape[0] // SUBLANE, SUBLANE, x.shape[1]), axis=0)


MOBA_HPS = 4


def _moba_kernel(*refs, n_blk):
    hps = MOBA_HPS
    q_refs, k_refs, v_refs = refs[:hps], refs[hps:2 * hps], refs[2 * hps:3 * hps]
    gq_ref, gk_ref, o_ref, kn, vt, kmean, chosen, s_scr, s_own, m_acc, l_acc, o_acc = refs[3 * hps:]
    qi = pl.program_id(2)
    blk = MOBA_BLOCK
    grp = MOBA_GROUP
    heads = range(hps)

    @pl.when(qi == 0)
    def _():
        for hh in heads:
            for n in range(n_blk):
                kb = _rms_rows(k_refs[hh][n * blk:(n + 1) * blk, :].astype(F32), gk_ref[...])
                kn[hh, n * blk:(n + 1) * blk, :] = kb.astype(BF16)
                kmean[hh, n:n + 1, :] = jnp.mean(kb, axis=0, keepdims=True)
                vt[hh, n] = v_refs[hh][n * blk:(n + 1) * blk, :].astype(F32).T.astype(BF16)

    brow = lax.broadcasted_iota(jnp.int32, (n_blk, blk), 0)
    past = brow < qi
    k_i = lax.broadcasted_iota(jnp.int32, (blk, blk), 0)
    q_i = lax.broadcasted_iota(jnp.int32, (blk, blk), 1)
    own = pl.multiple_of(qi * blk, blk)
    q_bf = []
    for hh in heads:
        qn = _rms_rows(q_refs[hh][...].astype(F32), gq_ref[...])
        q_bf.append((qn * (MOBA_HD ** -0.5 * LOG2_E)).astype(BF16))
        sb = lax.dot_general(kmean[hh], qn, NT_DIMS, precision=lax.Precision.HIGHEST,
                             preferred_element_type=F32)
        sb = jnp.where(past, sb, NEG_INF)
        rank = jnp.zeros((n_blk, blk), F32)
        for m in range(n_blk):
            cm = sb[m:m + 1, :]
            ge = jnp.where(cm >= sb, 1.0, 0.0)
            gt = jnp.where(cm > sb, 1.0, 0.0)
            rank = rank + jnp.where(brow > m, ge, gt)
        chosen[hh] = jnp.where(past, jnp.where(rank < float(MOBA_TOPK), 1.0, 0.0), 0.0)
        so = lax.dot_general(kn[hh, pl.ds(own, blk), :], q_bf[hh], NT_DIMS, preferred_element_type=F32)
        so = jnp.where(k_i <= q_i, so, NEG_INF)
        s_own[hh] = so
        m_acc[hh] = _fold_rows(so, jnp.max)

    def for_past_blocks(visit):
        n_full = qi // grp

        def body(g, c):
            visit(g * grp, grp)
            return c

        lax.fori_loop(0, n_full, body, 0)
        for rem in range(1, grp):
            @pl.when(qi % grp == rem)
            def _():
                visit(n_full * grp, rem)

    def scores(first, count):
        start = pl.multiple_of(first * blk, blk)
        for hh in heads:
            sg = lax.dot_general(kn[hh, pl.ds(start, count * blk), :], q_bf[hh], NT_DIMS,
                                 preferred_element_type=F32)
            m_run = m_acc[hh]
            for b in range(count):
                sn = jnp.where(chosen[hh, pl.ds(first + b, 1), :] > 0.0, sg[b * blk:(b + 1) * blk, :], NEG_INF)
                s_scr[hh, first + b] = sn
                m_run = jnp.maximum(m_run, _fold_rows(sn, jnp.max))
            m_acc[hh] = m_run

    for_past_blocks(scores)
    m_rows = [jnp.max(m_acc[hh], axis=0, keepdims=True) for hh in heads]

    for hh in heads:
        p_own = jnp.exp2(s_own[hh] - m_rows[hh])
        l_acc[hh] = _fold_rows(p_own, jnp.sum)
        o_acc[hh] = jnp.dot(vt[hh, qi], p_own.astype(BF16), preferred_element_type=F32)

    def values(first, count):
        for hh in heads:
            l_run = l_acc[hh]
            ps, vs = [], []
            for b in range(count):
                pn = jnp.exp2(s_scr[hh, first + b] - m_rows[hh])
                l_run = l_run + _fold_rows(pn, jnp.sum)
                ps.append(pn.astype(BF16))
                vs.append(vt[hh, first + b])
            l_acc[hh] = l_run
            o_acc[hh] += jnp.dot(jnp.concatenate(vs, axis=1), jnp.concatenate(ps, axis=0),
                                 preferred_element_type=F32)

    for_past_blocks(values)
    for hh in heads:
        l_row = jnp.sum(l_acc[hh], axis=0, keepdims=True)
        o_ref[:, hh * MOBA_HD:(hh + 1) * MOBA_HD] = (o_acc[hh] / l_row).T.astype(o_ref.dtype)


def _moba(z3, gq, gk):
    b_sz, s, _ = z3.shape
    n_blk = s // MOBA_BLOCK
    hd = MOBA_HD
    hps = MOBA_HPS

    def head_spec(rows, off, hh, row_index):
        return pl.BlockSpec((None, rows, hd), lambda b, h, i: (b, row_index(i), off // hd + h * hps + hh))

    q_specs = [head_spec(MOBA_BLOCK, OFF_MQ, hh, lambda i: i) for hh in range(hps)]
    k_specs = [head_spec(s, OFF_MK, hh, lambda i: 0) for hh in range(hps)]
    v_specs = [head_spec(s, OFF_MV, hh, lambda i: 0) for hh in range(hps)]
    vec = pl.BlockSpec((1, hd), lambda b, h, i: (0, 0))
    return pl.pallas_call(
        functools.partial(_moba_kernel, n_blk=n_blk),
        out_shape=jax.ShapeDtypeStruct((b_sz, s, MOBA_HEADS * hd), BF16),
        grid=(b_sz, MOBA_HEADS // hps, n_blk),
        in_specs=q_specs + k_specs + v_specs + [vec, vec],
        out_specs=pl.BlockSpec((None, MOBA_BLOCK, hps * hd), lambda b, h, i: (b, i, h)),
        scratch_shapes=[
            pltpu.VMEM((hps, s, hd), BF16),
            pltpu.VMEM((hps, n_blk, hd, MOBA_BLOCK), BF16),
            pltpu.VMEM((hps, n_blk, hd), F32),
            pltpu.VMEM((hps, n_blk, MOBA_BLOCK), F32),
            pltpu.VMEM((hps, n_blk, MOBA_BLOCK, MOBA_BLOCK), F32),
            pltpu.VMEM((hps, MOBA_BLOCK, MOBA_BLOCK), F32),
            pltpu.VMEM((hps, SUBLANE, MOBA_BLOCK), F32),
            pltpu.VMEM((hps, SUBLANE, MOBA_BLOCK), F32),
            pltpu.VMEM((hps, hd, MOBA_BLOCK), F32),
        ],
        compiler_params=_cparams(("arbitrary", "arbitrary", "arbitrary")),
        name="moba",
    )(*([z3] * (3 * hps)), gq, gk)


MERGE_COLS = 512


def _merge_kernel(a_ref, gl_ref, mo_ref, gt_ref, x_ref, wc_ref, wg_ref, wm_ref, wo_ref, gn_ref,
                  wup_ref, wdn_ref, o_ref, hn_ref, wup_o, wdn_o, mg):
    d = D_MODEL
    wup_o[...] = wup_ref[...].astype(wup_o.dtype)
    wdn_o[...] = wdn_ref[...].astype(wdn_o.dtype)
    for c0 in range(0, d, MERGE_COLS):
        cs = slice(c0, c0 + MERGE_COLS)
        ya = jnp.dot(a_ref[...], wc_ref[:, cs], preferred_element_type=F32)
        yb = jnp.dot(gl_ref[...], wg_ref[:, cs], preferred_element_type=F32)
        yc = jnp.dot(mo_ref[...], wm_ref[:, cs], preferred_element_type=F32)
        g0 = gt_ref[:, c0:c0 + MERGE_COLS].astype(F32)
        g1 = gt_ref[:, d + c0:d + c0 + MERGE_COLS].astype(F32)
        g2 = gt_ref[:, 2 * d + c0:2 * d + c0 + MERGE_COLS].astype(F32)
        mg[:, cs] = (g0 * ya + g1 * yb + g2 * yc).astype(BF16)
    x_new = x_ref[...] + jnp.dot(mg[...], wo_ref[...], preferred_element_type=F32)
    o_ref[...] = x_new
    hn_ref[...] = _rms_rows(x_new, gn_ref[...]).astype(hn_ref.dtype)


def _resident(w, layer):
    if w.ndim == 2:
        return pl.BlockSpec(w.shape, lambda i: (0, 0), pipeline_mode=pl.Buffered(1))
    return pl.BlockSpec((None,) + w.shape[1:], lambda i: (layer, 0, 0), pipeline_mode=pl.Buffered(1))


def _merge(a_act, o_gla, o_moba, gates, x, wc, wg, wm, wo, g_next, w_up, w_down, layer, *, tm=256):
    m, d = x.shape
    steps = m // tm
    up_rows, dn_rows = w_up.shape[1] // steps, w_down.shape[1] // steps

    def rows(width):
        return pl.BlockSpec((tm, width), lambda i: (i, 0))

    return pl.pallas_call(
        _merge_kernel,
        out_shape=(jax.ShapeDtypeStruct((m, d), F32), jax.ShapeDtypeStruct((m, d), BF16),
                   jax.ShapeDtypeStruct(w_up.shape[1:], BF16), jax.ShapeDtypeStruct(w_down.shape[1:], BF16)),
        grid=(steps,),
        in_specs=[rows(a_act.shape[1]), rows(o_gla.shape[1]), rows(o_moba.shape[1]), rows(gates.shape[1]),
                  rows(d), _resident(wc, layer), _resident(wg, layer), _resident(wm, layer),
                  _resident(wo, layer), _layer_vec(d, layer),
                  pl.BlockSpec((None, up_rows, w_up.shape[2]), lambda i: (layer, i, 0)),
                  pl.BlockSpec((None, dn_rows, w_down.shape[2]), lambda i: (layer, i, 0))],
        out_specs=(rows(d), rows(d),
                   pl.BlockSpec((up_rows, w_up.shape[2]), lambda i: (i, 0)),
                   pl.BlockSpec((dn_rows, w_down.shape[2]), lambda i: (i, 0))),
        scratch_shapes=[pltpu.VMEM((tm, d), BF16)],
        compiler_params=_cparams(("arbitrary",)),
        name="merge",
    )(a_act, o_gla, o_moba, gates, x, wc, wg, wm, wo, g_next, w_up, w_down)


def _shift_rows(u, k, carry):
    rolled = pltpu.roll(u, k, 0)
    head = jnp.where(lax.broadcasted_iota(jnp.int32, carry.shape, 0) < k,
                     pltpu.roll(carry, k, 0), rolled[:SUBLANE, :])
    return jnp.concatenate([head, rolled[SUBLANE:, :]], axis=0)


def _ffn_kernel(h_ref, wug_ref, wuv_ref, cwg_ref, cwv_ref, wd_ref, o_ref, cg, cv, *, tiles_per_seq):
    i = pl.program_id(0)
    j = pl.program_id(1)
    tm = h_ref.shape[0]

    @pl.when(j == 0)
    def _():
        o_ref[...] = jnp.zeros(o_ref.shape, o_ref.dtype)

    @pl.when(i % tiles_per_seq == 0)
    def _():
        cg[j] = jnp.zeros(cg.shape[1:], F32)
        cv[j] = jnp.zeros(cv.shape[1:], F32)

    h = h_ref[...]

    def conv(w_ref, cw_ref, carry_ref):
        u = jnp.dot(h, w_ref[...], preferred_element_type=F32)
        carry = carry_ref[j]
        out = (u * cw_ref[2:3, :] + _shift_rows(u, 1, carry) * cw_ref[1:2, :]
               + _shift_rows(u, 2, carry) * cw_ref[0:1, :])
        carry_ref[j] = u[tm - SUBLANE:, :]
        return out

    ug = conv(wug_ref, cwg_ref, cg)
    uv = conv(wuv_ref, cwv_ref, cv)
    act = (_silu(ug) * uv).astype(BF16)
    o_ref[...] += jnp.dot(act, wd_ref[...], preferred_element_type=F32)


def _ffn(h, w_up, conv_w, w_down, layer, *, seq, tm=1024, tf=512):
    m, d = h.shape
    f = w_down.shape[0]
    nf = f // tf
    return pl.pallas_call(
        functools.partial(_ffn_kernel, tiles_per_seq=seq // tm),
        out_shape=jax.ShapeDtypeStruct((m, d), F32),
        grid=(m // tm, nf),
        in_specs=[
            pl.BlockSpec((tm, d), lambda i, j: (i, 0)),
            pl.BlockSpec((d, tf), lambda i, j: (0, j)),
            pl.BlockSpec((d, tf), lambda i, j: (0, j + nf)),
            pl.BlockSpec((None, FFN_CONV_WIDTH, tf), lambda i, j: (layer, 0, j)),
            pl.BlockSpec((None, FFN_CONV_WIDTH, tf), lambda i, j: (layer, 0, j + nf)),
            pl.BlockSpec((tf, d), lambda i, j: (j, 0)),
        ],
        out_specs=pl.BlockSpec((tm, d), lambda i, j: (i, 0)),
        scratch_shapes=[
            pltpu.VMEM((nf, SUBLANE, tf), F32),
            pltpu.VMEM((nf, SUBLANE, tf), F32),
        ],
        compiler_params=_cparams(("arbitrary", "arbitrary")),
        name="conv_ffn",
    )(h, w_up, w_up, conv_w, conv_w, w_down)


PLE_COLS = 512


def _ple_kernel(x_ref, y_ref, p_ref, g_ref, wg_ref, wp_ref, *rest):
    h_scr = rest[-1]
    o_ref = rest[0] if len(rest) == 2 else rest[1]
    x2 = x_ref[...] + y_ref[...]
    o_ref[...] = x2
    h_scr[...] = _rms_rows(x2, g_ref[...]).astype(h_scr.dtype)
    p_bf = p_ref[...].astype(BF16)
    for c0 in range(0, D_MODEL, PLE_COLS):
        cs = slice(c0, c0 + PLE_COLS)
        gate = _sigmoid(jnp.dot(h_scr[...], wg_ref[:, cs], preferred_element_type=F32))
        emb = jnp.dot(p_bf, wp_ref[:, cs], preferred_element_type=F32)
        o_ref[:, cs] = o_ref[:, cs] + gate * emb
    if len(rest) == 4:
        gn_ref, _, hn_ref, _ = rest
        hn_ref[...] = _rms_rows(o_ref[...], gn_ref[...]).astype(hn_ref.dtype)


def _ple(x, y, p, g_ple, w_gate, w_ple, layer, g_next, *, tm=512):
    m, d = x.shape
    rows = lambda width: pl.BlockSpec((tm, width), lambda i: (i, 0))
    in_specs = [rows(d), rows(d), pl.BlockSpec((None, tm, PLE_DIM), lambda i: (layer, i, 0)),
                _layer_vec(d, layer), _resident(w_gate, layer), _resident(w_ple, layer)]
    args = [x, y, p, g_ple, w_gate, w_ple]
    out_shape = [jax.ShapeDtypeStruct((m, d), F32)]
    out_specs = [rows(d)]
    if g_next is not None:
        in_specs.append(_layer_vec(d, layer + 1))
        args.append(g_next)
        out_shape.append(jax.ShapeDtypeStruct((m, d), BF16))
        out_specs.append(rows(d))
    return pl.pallas_call(
        _ple_kernel,
        out_shape=tuple(out_shape),
        grid=(m // tm,),
        in_specs=in_specs,
        out_specs=tuple(out_specs),
        scratch_shapes=[pltpu.VMEM((tm, d), BF16)],
        compiler_params=_cparams(("arbitrary",)),
        name="ple",
    )(*args)


def kernel(x, p, norm_mix_g, w_in, conv_w, conv_b, conv_ln_g, conv_ln_b, w_conv_out, gla_w_a2, gla_b_a,
           gla_norm_g, w_gla_out, moba_q_norm_g, moba_k_norm_g, w_moba_out, w_gate, b_gate, w_out,
           norm_ffn_g, w_up, ffn_conv_w, w_down, norm_ple_g, w_ple_gate, w_ple):
    b_sz, s, d = x.shape
    depth = w_in.shape[0]
    m = b_sz * s
    xf = x.reshape(m, d)
    p2 = p.reshape(depth, m, PLE_DIM)
    row = lambda v: v.reshape(1, -1)
    stack_vec = lambda v: v.reshape(depth, 1, -1)
    w_in_t = jnp.swapaxes(w_in, 1, 2)
    w_ple_b = w_ple.astype(BF16)
    w_a2p = jnp.concatenate(
        [gla_w_a2, jnp.zeros((depth, GATE_PAD - GLA_GATE_RANK, GLA_QK), F32)], axis=1).astype(BF16)
    g_mix, g_ffn, g_ple, b_gate3 = (stack_vec(v) for v in (norm_mix_g, norm_ffn_g, norm_ple_g, b_gate))
    conv_b3, conv_ln_g3, conv_ln_b3 = (stack_vec(v) for v in (conv_b, conv_ln_g, conv_ln_b))

    h = _rmsnorm(xf, g_mix, 0)
    for i in range(depth):
        z3 = _proj_nt(h, w_in_t, i, 0, W_CONV_GLA, tm=2048, tn=512,
                      name="in_proj_conv_gla").reshape(b_sz, s, W_CONV_GLA)
        zg3 = _proj_nt(h, w_in_t, i, W_CONV_GLA, GATE_PAD, tm=2048, tn=GATE_PAD,
                       name="in_proj_gate").reshape(b_sz, s, GATE_PAD)
        zm3 = _proj_nt(h, w_in_t, i, W_CONV_GLA + GLA_GATE_RANK, W_MOBA, tm=2048, tn=1024,
                       name="in_proj_moba").reshape(b_sz, s, W_MOBA)
        gates = _proj_sigmoid(h, w_gate, b_gate3, i, tm=2048, tn=1024, name="gate_proj")
        a_act = _conformer_conv(z3, conv_w, conv_b3, conv_ln_g3, conv_ln_b3, i)
        o_gla, wc_b, wg_b, wm_b, wo_b, w_pg_b = _gla(
            z3, zg3, w_a2p[i], row(gla_b_a[i]), row(gla_norm_g[i]),
            (w_conv_out, w_gla_out, w_moba_out, w_out, w_ple_gate), i)
        o_moba = _moba(zm3, row(moba_q_norm_g[i]), row(moba_k_norm_g[i]))
        xf, h, w_up_b, w_down_b = _merge(a_act.reshape(m, -1), o_gla.reshape(m, -1), o_moba.reshape(m, -1),
                                         gates, xf, wc_b, wg_b, wm_b, wo_b, g_ffn, w_up, w_down, i)
        y = _ffn(h, w_up_b, ffn_conv_w, w_down_b, i, seq=s)
        if i + 1 < depth:
            xf, h = _ple(xf, y, p2, g_ple, w_pg_b, w_ple_b, i, g_mix)
        else:
            (xf,) = _ple(xf, y, p2, g_ple, w_pg_b, w_ple_b, i, None)
    return xf.reshape(b_sz, s, d)
```

```python
import functools

import jax
import jax.numpy as jnp
from jax import lax
from jax.experimental import pallas as pl
from jax.experimental.pallas import tpu as pltpu

F32 = jnp.float32
BF16 = jnp.bfloat16

D_MODEL = 2048
PLE_DIM = 256
EPS = 1e-6
CONV_CH = 512
CONV_WIDTH = 31
GLA_HEADS = 4
GLA_DV = 128
GLA_DK = 64
GLA_GATE_RANK = 16
GLA_TAU = 16.0
GLA_CHUNK = 64
GLA_SUB = 16
MOBA_HEADS = 8
MOBA_HD = 128
MOBA_BLOCK = 256
MOBA_TOPK = 3
FFN_DIM = 5632
FFN_CONV_WIDTH = 3

LANE = 128
SUBLANE = 8
GATE_PAD = LANE
OFF_GLU = 0
OFF_GQ = OFF_GLU + 2 * CONV_CH
OFF_GK = OFF_GQ + GLA_HEADS * GLA_DK
OFF_GV = OFF_GK + GLA_HEADS * GLA_DK
OFF_GR = OFF_GV + GLA_HEADS * GLA_DV
W_CONV_GLA = OFF_GR + GLA_HEADS * GLA_DV
OFF_MQ = 0
OFF_MK = OFF_MQ + MOBA_HEADS * MOBA_HD
OFF_MV = OFF_MK + MOBA_HEADS * MOBA_HD
W_MOBA = OFF_MV + MOBA_HEADS * MOBA_HD
NT_DIMS = (((1,), (1,)), ((), ()))

VMEM_LIMIT = 56 * 1024 * 1024


def _cparams(sem):
    return pltpu.CompilerParams(dimension_semantics=sem, vmem_limit_bytes=VMEM_LIMIT)


def _sigmoid(x):
    return 1.0 / (1.0 + jnp.exp(-x))


def _silu(x):
    return x * _sigmoid(x)


def _rms_rows(x, g):
    ms = jnp.mean(x * x, axis=-1, keepdims=True)
    return x * lax.rsqrt(ms + EPS) * g


def _layer_vec(width, layer):
    return pl.BlockSpec((None, 1, width), lambda i: (layer, 0, 0))


def _rmsnorm_kernel(x_ref, g_ref, o_ref):
    o_ref[...] = _rms_rows(x_ref[...], g_ref[...]).astype(o_ref.dtype)


def _rmsnorm(x, g_stack, layer, *, tm=1024):
    m, d = x.shape
    return pl.pallas_call(
        _rmsnorm_kernel,
        out_shape=jax.ShapeDtypeStruct((m, d), BF16),
        grid=(m // tm,),
        in_specs=[pl.BlockSpec((tm, d), lambda i: (i, 0)), _layer_vec(d, layer)],
        out_specs=pl.BlockSpec((tm, d), lambda i: (i, 0)),
        compiler_params=_cparams(("arbitrary",)),
        name="rmsnorm",
    )(x, g_stack)


def _proj_nt_kernel(h_ref, wt_ref, o_ref):
    acc = lax.dot_general(h_ref[...], wt_ref[0].astype(BF16), NT_DIMS, preferred_element_type=F32)
    o_ref[...] = acc.astype(o_ref.dtype)


def _proj_nt(h, wt_stack, layer, row0, n, *, tm, tn, name):
    m, d = h.shape
    return pl.pallas_call(
        _proj_nt_kernel,
        out_shape=jax.ShapeDtypeStruct((m, n), BF16),
        grid=(m // tm, n // tn),
        in_specs=[pl.BlockSpec((tm, d), lambda i, j: (i, 0)),
                  pl.BlockSpec((pl.Element(1), pl.Element(tn), pl.Element(d)),
                               lambda i, j: (layer, pl.multiple_of(row0 + j * tn, SUBLANE), 0))],
        out_specs=pl.BlockSpec((tm, tn), lambda i, j: (i, j)),
        compiler_params=_cparams(("arbitrary", "arbitrary")),
        name=name,
    )(h, wt_stack)


def _proj_sigmoid_kernel(h_ref, w_ref, b_ref, o_ref):
    acc = jnp.dot(h_ref[...], w_ref[...].astype(BF16), preferred_element_type=F32)
    o_ref[...] = _sigmoid(acc + b_ref[...]).astype(o_ref.dtype)


def _proj_sigmoid(h, w_stack, b_stack, layer, *, tm, tn, name):
    m, d = h.shape
    n = w_stack.shape[2]
    return pl.pallas_call(
        _proj_sigmoid_kernel,
        out_shape=jax.ShapeDtypeStruct((m, n), BF16),
        grid=(m // tm, n // tn),
        in_specs=[pl.BlockSpec((tm, d), lambda i, j: (i, 0)),
                  pl.BlockSpec((None, d, tn), lambda i, j: (layer, 0, j)),
                  pl.BlockSpec((None, 1, tn), lambda i, j: (layer, 0, j))],
        out_specs=pl.BlockSpec((tm, tn), lambda i, j: (i, j)),
        compiler_params=_cparams(("arbitrary", "arbitrary")),
        name=name,
    )(h, w_stack, b_stack)


CONV_HALO = 32
CONV_ROWS = 64


def _conv_kernel(z_ref, w_ref, cb_ref, lg_ref, lb_ref, o_ref, abuf, sh, *, ts):
    t = pl.program_id(1)

    @pl.when(t == 0)
    def _():
        abuf[0:CONV_HALO, :] = jnp.zeros((CONV_HALO, CONV_CH), F32)

    @pl.when(t > 0)
    def _():
        abuf[0:CONV_HALO, :] = abuf[ts:ts + CONV_HALO, :]

    z = z_ref[...].astype(F32)
    abuf[CONV_HALO:CONV_HALO + ts, :] = z[:, :CONV_CH] * _sigmoid(z[:, CONV_CH:])

    sh_rows = sh.shape[1]
    for b in range(1, SUBLANE):
        sh[b] = abuf[b:b + sh_rows, :]

    first = CONV_HALO - (CONV_WIDTH - 1)
    for r0 in range(0, ts, CONV_ROWS):
        acc = abuf[r0 + CONV_HALO:r0 + CONV_HALO + CONV_ROWS, :] * w_ref[CONV_WIDTH - 1:CONV_WIDTH, :]
        for j in range(CONV_WIDTH - 1):
            res = (first + j) % SUBLANE
            base = r0 + first + j - res
            rows = abuf[base:base + CONV_ROWS, :] if res == 0 else sh[res, base:base + CONV_ROWS, :]
            acc = acc + rows * w_ref[j:j + 1, :]
        acc = acc + cb_ref[...]
        mu = jnp.mean(acc, axis=-1, keepdims=True)
        xc = acc - mu
        var = jnp.mean(xc * xc, axis=-1, keepdims=True)
        y = xc * lax.rsqrt(var + EPS) * lg_ref[...] + lb_ref[...]
        o_ref[r0:r0 + CONV_ROWS, :] = _silu(y).astype(o_ref.dtype)


def _conformer_conv(z3, conv_w, conv_b, ln_g, ln_b, layer, *, ts=512):
    b_sz, s, _ = z3.shape

    def par(a):
        return pl.BlockSpec((None,) + a.shape[1:], lambda b, t: (layer, 0, 0))

    return pl.pallas_call(
        functools.partial(_conv_kernel, ts=ts),
        out_shape=jax.ShapeDtypeStruct((b_sz, s, CONV_CH), BF16),
        grid=(b_sz, s // ts),
        in_specs=[
            pl.BlockSpec((None, ts, 2 * CONV_CH), lambda b, t: (b, t, OFF_GLU // (2 * CONV_CH))),
            par(conv_w), par(conv_b), par(ln_g), par(ln_b),
        ],
        out_specs=pl.BlockSpec((None, ts, CONV_CH), lambda b, t: (b, t, 0)),
        scratch_shapes=[pltpu.VMEM((CONV_HALO + ts, CONV_CH), F32),
                        pltpu.VMEM((SUBLANE, CONV_HALO + ts - SUBLANE, CONV_CH), F32)],
        compiler_params=_cparams(("arbitrary", "arbitrary")),
        name="conformer_conv",
    )(z3, conv_w, conv_b, ln_g, ln_b)


LOG2_E = 1.4426950408889634
GLA_QK = GLA_HEADS * GLA_DK
GLA_V = GLA_HEADS * GLA_DV


GLA_CPS = 4


def _gla_kernel(*refs, n_narrow):
    q_ref, k_ref, v_ref, r_ref, al_ref, wa_ref, ba_ref, ng_ref = refs[:8]
    w_f32 = refs[8:8 + n_narrow]
    o_ref = refs[8 + n_narrow]
    w_bf16 = refs[9 + n_narrow:9 + 2 * n_narrow]
    st, qf, kf, vf, bcs, pbig, intra = refs[9 + 2 * n_narrow:]

    for src, dst in zip(w_f32, w_bf16):
        dst[...] = src[...].astype(dst.dtype)

    @pl.when(pl.program_id(1) == 0)
    def _():
        st[...] = jnp.zeros(st.shape, F32)

    for ci in range(GLA_CPS):
        rs = slice(ci * GLA_CHUNK, (ci + 1) * GLA_CHUNK)
        _gla_chunk(q_ref.at[rs, :], k_ref.at[rs, :], v_ref.at[rs, :], r_ref.at[rs, :], al_ref.at[rs, :],
                   wa_ref, ba_ref, ng_ref, o_ref.at[rs, :],
                   st, qf.at[ci], kf.at[ci], vf.at[ci], bcs.at[ci], pbig.at[ci], intra.at[ci])


def _gla_chunk(q_ref, k_ref, v_ref, r_ref, al_ref, wa_ref, ba_ref, ng_ref, o_ref,
               st, qf, kf, vf, bcs, pbig, intra):
    c = GLA_CHUNK

    xa = jnp.dot(al_ref[...], wa_ref[...], preferred_element_type=F32) + ba_ref[...]
    la = (jnp.minimum(xa, 0.0) - jnp.log(1.0 + jnp.exp(-jnp.abs(xa)))) * (1.0 / GLA_TAU)
    row = lax.broadcasted_iota(jnp.int32, (c, c), 0)
    col = lax.broadcasted_iota(jnp.int32, (c, c), 1)
    tri = jnp.where(row >= col, 1.0, 0.0).astype(BF16)
    la_hi = la.astype(BF16)
    la_lo = (la - la_hi.astype(F32)).astype(BF16)
    bc = (jnp.dot(tri, la_hi, preferred_element_type=F32)
          + jnp.dot(tri, la_lo, preferred_element_type=F32))

    q = q_ref[...].astype(F32) * (GLA_DK ** -0.5)
    k = k_ref[...].astype(F32)
    qf[...] = q
    kf[...] = k
    vf[...] = v_ref[...].astype(F32)
    bcs[...] = bc * LOG2_E

    b_last = bc[c - 1:c, :]
    q_in = (q * jnp.exp(bc)).astype(BF16)
    k_dec = (k * jnp.exp(b_last - bc)).astype(BF16)
    e_last = jnp.exp(b_last)
    v_bf = v_ref[...]

    srow = lax.broadcasted_iota(jnp.int32, (GLA_QK, GLA_V), 0) // GLA_DK
    scol = lax.broadcasted_iota(jnp.int32, (GLA_QK, GLA_V), 1) // GLA_DV
    sel = jnp.where(srow == scol, 1.0, 0.0).astype(BF16)

    for j in range(c // GLA_SUB):
        t0 = j * GLA_SUB
        tj = c - t0
        q_rows = qf[t0:c, :]
        bc_rows = bcs[t0:c, :]
        t_loc = lax.broadcasted_iota(jnp.int32, (GLA_SUB, GLA_QK), 0)
        for sl in range(GLA_SUB):
            s = t0 + sl
            d = bc_rows - bcs[s:s + 1, :]
            d_own = jnp.minimum(d[:GLA_SUB, :], 0.0)
            d = d_own if tj == GLA_SUB else jnp.concatenate([d_own, d[GLA_SUB:, :]], axis=0)
            p = (q_rows * kf[s:s + 1, :]) * jnp.exp2(d)
            p_own = jnp.where(t_loc >= sl, p[:GLA_SUB, :], 0.0)
            p = p_own if tj == GLA_SUB else jnp.concatenate([p_own, p[GLA_SUB:, :]], axis=0)
            pbig[sl * tj:(sl + 1) * tj, :] = p.astype(BF16)
        r_all = jnp.dot(pbig[0:GLA_SUB * tj, :], sel, preferred_element_type=F32)
        acc = jnp.zeros((tj, GLA_V), F32)
        for sl in range(GLA_SUB):
            s = t0 + sl
            acc = acc + r_all[sl * tj:(sl + 1) * tj, :] * vf[s:s + 1, :]
        if j == 0:
            intra[...] = acc
        else:
            intra[t0:c, :] = intra[t0:c, :] + acc

    for h in range(GLA_HEADS):
        ks = slice(h * GLA_DK, (h + 1) * GLA_DK)
        vs = slice(h * GLA_DV, (h + 1) * GLA_DV)
        st_h = st[h]
        inter = lax.dot_general(q_in[:, ks], st_h.astype(BF16), (((1,), (1,)), ((), ())),
                                preferred_element_type=F32)
        upd = lax.dot_general(v_bf[:, vs], k_dec[:, ks], (((0,), (0,)), ((), ())),
                              preferred_element_type=F32)
        st[h] = st_h * e_last[:, ks] + upd
        o_h = inter + intra[:, vs]
        y = _rms_rows(o_h, ng_ref[...])
        o_ref[:, vs] = (y * _silu(r_ref[:, vs].astype(F32))).astype(o_ref.dtype)


def _gla(z3, zg3, w_a2p, b_a, norm_g, narrow, layer):
    b_sz, s, _ = z3.shape
    c = GLA_CHUNK
    rows = c * GLA_CPS
    cps = GLA_CPS
    t_steps = s // rows
    n_steps = b_sz * t_steps

    def zspec(width, off):
        return pl.BlockSpec((None, rows, width), lambda b, t: (b, t, off // width))

    slab = [w.shape[1] // n_steps for w in narrow]
    return pl.pallas_call(
        functools.partial(_gla_kernel, n_narrow=len(narrow)),
        out_shape=(jax.ShapeDtypeStruct((b_sz, s, GLA_V), BF16),
                   *[jax.ShapeDtypeStruct(w.shape[1:], BF16) for w in narrow]),
        grid=(b_sz, t_steps),
        in_specs=[
            zspec(GLA_QK, OFF_GQ), zspec(GLA_QK, OFF_GK), zspec(GLA_V, OFF_GV), zspec(GLA_V, OFF_GR),
            zspec(GATE_PAD, 0),
            pl.BlockSpec((GATE_PAD, GLA_QK), lambda b, t: (0, 0)),
            pl.BlockSpec((1, GLA_QK), lambda b, t: (0, 0)),
            pl.BlockSpec((1, GLA_DV), lambda b, t: (0, 0)),
            *[pl.BlockSpec((None, r, w.shape[2]), lambda b, t: (layer, b * t_steps + t, 0))
              for r, w in zip(slab, narrow)],
        ],
        out_specs=(pl.BlockSpec((None, rows, GLA_V), lambda b, t: (b, t, 0)),
                   *[pl.BlockSpec((r, w.shape[2]), lambda b, t: (b * t_steps + t, 0))
                     for r, w in zip(slab, narrow)]),
        scratch_shapes=[
            pltpu.VMEM((GLA_HEADS, GLA_DV, GLA_DK), F32),
            pltpu.VMEM((cps, c, GLA_QK), F32),
            pltpu.VMEM((cps, c, GLA_QK), F32),
            pltpu.VMEM((cps, c, GLA_V), F32),
            pltpu.VMEM((cps, c, GLA_QK), F32),
            pltpu.VMEM((cps, GLA_SUB * c, GLA_QK), BF16),
            pltpu.VMEM((cps, c, GLA_V), F32),
        ],
        compiler_params=_cparams(("arbitrary", "arbitrary")),
        name="gla",
    )(z3, z3, z3, z3, zg3, w_a2p, b_a, norm_g, *narrow)


NEG_INF = float("-inf")
MOBA_GROUP = 4


def _fold_rows(x, op):
    return op(x.reshape(x.shape[0] // SUBLANE, SUBLANE, x.shape[1]), axis=0)


MOBA_HPS = 4


def _moba_kernel(*refs, n_blk):
    hps = MOBA_HPS
    q_refs, k_refs, v_refs = refs[:hps], refs[hps:2 * hps], refs[2 * hps:3 * hps]
    gq_ref, gk_ref, o_ref, kn, vt, kmean, chosen, s_scr, s_own, m_acc, l_acc, o_acc = refs[3 * hps:]
    qi = pl.program_id(2)
    blk = MOBA_BLOCK
    grp = MOBA_GROUP
    heads = range(hps)

    @pl.when(qi == 0)
    def _():
        for hh in heads:
            for n in range(n_blk):
                kb = _rms_rows(k_refs[hh][n * blk:(n + 1) * blk, :].astype(F32), gk_ref[...])
                kn[hh, n * blk:(n + 1) * blk, :] = kb.astype(BF16)
                kmean[hh, n:n + 1, :] = jnp.mean(kb, axis=0, keepdims=True)
                vt[hh, n] = v_refs[hh][n * blk:(n + 1) * blk, :].astype(F32).T.astype(BF16)

    brow = lax.broadcasted_iota(jnp.int32, (n_blk, blk), 0)
    past = brow < qi
    k_i = lax.broadcasted_iota(jnp.int32, (blk, blk), 0)
    q_i = lax.broadcasted_iota(jnp.int32, (blk, blk), 1)
    own = pl.multiple_of(qi * blk, blk)
    q_bf = []
    for hh in heads:
        qn = _rms_rows(q_refs[hh][...].astype(F32), gq_ref[...])
        q_bf.append((qn * (MOBA_HD ** -0.5 * LOG2_E)).astype(BF16))
        sb = lax.dot_general(kmean[hh], qn, NT_DIMS, precision=lax.Precision.HIGHEST,
                             preferred_element_type=F32)
        sb = jnp.where(past, sb, NEG_INF)
        rank = jnp.zeros((n_blk, blk), F32)
        for m in range(n_blk):
            cm = sb[m:m + 1, :]
            ge = jnp.where(cm >= sb, 1.0, 0.0)
            gt = jnp.where(cm > sb, 1.0, 0.0)
            rank = rank + jnp.where(brow > m, ge, gt)
        chosen[hh] = jnp.where(past, jnp.where(rank < float(MOBA_TOPK), 1.0, 0.0), 0.0)
        so = lax.dot_general(kn[hh, pl.ds(own, blk), :], q_bf[hh], NT_DIMS, preferred_element_type=F32)
        so = jnp.where(k_i <= q_i, so, NEG_INF)
        s_own[hh] = so
        m_acc[hh] = _fold_rows(so, jnp.max)

    def for_past_blocks(visit):
        n_full = qi // grp

        def body(g, c):
            visit(g * grp, grp)
            return c

        lax.fori_loop(0, n_full, body, 0)
        for rem in range(1, grp):
            @pl.when(qi % grp == rem)
            def _():
                visit(n_full * grp, rem)

    def scores(first, count):
        start = pl.multiple_of(first * blk, blk)
        for hh in heads:
            sg = lax.dot_general(kn[hh, pl.ds(start, count * blk), :], q_bf[hh], NT_DIMS,
                                 preferred_element_type=F32)
            m_run = m_acc[hh]
            for b in range(count):
                sn = jnp.where(chosen[hh, pl.ds(first + b, 1), :] > 0.0, sg[b * blk:(b + 1) * blk, :], NEG_INF)
                s_scr[hh, first + b] = sn
                m_run = jnp.maximum(m_run, _fold_rows(sn, jnp.max))
            m_acc[hh] = m_run

    for_past_blocks(scores)
    m_rows = [jnp.max(m_acc[hh], axis=0, keepdims=True) for hh in heads]

    for hh in heads:
        p_own = jnp.exp2(s_own[hh] - m_rows[hh])
        l_acc[hh] = _fold_rows(p_own, jnp.sum)
        o_acc[hh] = jnp.dot(vt[hh, qi], p_own.astype(BF16), preferred_element_type=F32)

    def values(first, count):
        for hh in heads:
            l_run = l_acc[hh]
            ps, vs = [], []
            for b in range(count):
                pn = jnp.exp2(s_scr[hh, first + b] - m_rows[hh])
                l_run = l_run + _fold_rows(pn, jnp.sum)
                ps.append(pn.astype(BF16))
                vs.append(vt[hh, first + b])
            l_acc[hh] = l_run
            o_acc[hh] += jnp.dot(jnp.concatenate(vs, axis=1), jnp.concatenate(ps, axis=0),
                                 preferred_element_type=F32)

    for_past_blocks(values)
    for hh in heads:
        l_row = jnp.sum(l_acc[hh], axis=0, keepdims=True)
        o_ref[:, hh * MOBA_HD:(hh + 1) * MOBA_HD] = (o_acc[hh] / l_row).T.astype(o_ref.dtype)


def _moba(z3, gq, gk):
    b_sz, s, _ = z3.shape
    n_blk = s // MOBA_BLOCK
    hd = MOBA_HD
    hps = MOBA_HPS

    def head_spec(rows, off, hh, row_index):
        return pl.BlockSpec((None, rows, hd), lambda b, h, i: (b, row_index(i), off // hd + h * hps + hh))

    q_specs = [head_spec(MOBA_BLOCK, OFF_MQ, hh, lambda i: i) for hh in range(hps)]
    k_specs = [head_spec(s, OFF_MK, hh, lambda i: 0) for hh in range(hps)]
    v_specs = [head_spec(s, OFF_MV, hh, lambda i: 0) for hh in range(hps)]
    vec = pl.BlockSpec((1, hd), lambda b, h, i: (0, 0))
    return pl.pallas_call(
        functools.partial(_moba_kernel, n_blk=n_blk),
        out_shape=jax.ShapeDtypeStruct((b_sz, s, MOBA_HEADS * hd), BF16),
        grid=(b_sz, MOBA_HEADS // hps, n_blk),
        in_specs=q_specs + k_specs + v_specs + [vec, vec],
        out_specs=pl.BlockSpec((None, MOBA_BLOCK, hps * hd), lambda b, h, i: (b, i, h)),
        scratch_shapes=[
            pltpu.VMEM((hps, s, hd), BF16),
            pltpu.VMEM((hps, n_blk, hd, MOBA_BLOCK), BF16),
            pltpu.VMEM((hps, n_blk, hd), F32),
            pltpu.VMEM((hps, n_blk, MOBA_BLOCK), F32),
            pltpu.VMEM((hps, n_blk, MOBA_BLOCK, MOBA_BLOCK), F32),
            pltpu.VMEM((hps, MOBA_BLOCK, MOBA_BLOCK), F32),
            pltpu.VMEM((hps, SUBLANE, MOBA_BLOCK), F32),
            pltpu.VMEM((hps, SUBLANE, MOBA_BLOCK), F32),
            pltpu.VMEM((hps, hd, MOBA_BLOCK), F32),
        ],
        compiler_params=_cparams(("arbitrary", "arbitrary", "arbitrary")),
        name="moba",
    )(*([z3] * (3 * hps)), gq, gk)


MERGE_COLS = 512


def _merge_kernel(a_ref, gl_ref, mo_ref, gt_ref, x_ref, wc_ref, wg_ref, wm_ref, wo_ref, gn_ref,
                  wup_ref, o_ref, hn_ref, wup_o, mg):
    d = D_MODEL
    wup_o[...] = wup_ref[...].astype(wup_o.dtype)
    for c0 in range(0, d, MERGE_COLS):
        cs = slice(c0, c0 + MERGE_COLS)
        ya = jnp.dot(a_ref[...], wc_ref[:, cs], preferred_element_type=F32)
        yb = jnp.dot(gl_ref[...], wg_ref[:, cs], preferred_element_type=F32)
        yc = jnp.dot(mo_ref[...], wm_ref[:, cs], preferred_element_type=F32)
        g0 = gt_ref[:, c0:c0 + MERGE_COLS].astype(F32)
        g1 = gt_ref[:, d + c0:d + c0 + MERGE_COLS].astype(F32)
        g2 = gt_ref[:, 2 * d + c0:2 * d + c0 + MERGE_COLS].astype(F32)
        mg[:, cs] = (g0 * ya + g1 * yb + g2 * yc).astype(BF16)
    x_new = x_ref[...] + jnp.dot(mg[...], wo_ref[...], preferred_element_type=F32)
    o_ref[...] = x_new
    hn_ref[...] = _rms_rows(x_new, gn_ref[...]).astype(hn_ref.dtype)


def _resident(w, layer):
    if w.ndim == 2:
        return pl.BlockSpec(w.shape, lambda i: (0, 0), pipeline_mode=pl.Buffered(1))
    return pl.BlockSpec((None,) + w.shape[1:], lambda i: (layer, 0, 0), pipeline_mode=pl.Buffered(1))


def _merge(a_act, o_gla, o_moba, gates, x, wc, wg, wm, wo, g_next, w_up, layer, *, tm=256):
    m, d = x.shape
    steps = m // tm
    up_rows = w_up.shape[1] // steps

    def rows(width):
        return pl.BlockSpec((tm, width), lambda i: (i, 0))

    return pl.pallas_call(
        _merge_kernel,
        out_shape=(jax.ShapeDtypeStruct((m, d), F32), jax.ShapeDtypeStruct((m, d), BF16),
                   jax.ShapeDtypeStruct(w_up.shape[1:], BF16)),
        grid=(steps,),
        in_specs=[rows(a_act.shape[1]), rows(o_gla.shape[1]), rows(o_moba.shape[1]), rows(gates.shape[1]),
                  rows(d), _resident(wc, layer), _resident(wg, layer), _resident(wm, layer),
                  _resident(wo, layer), _layer_vec(d, layer),
                  pl.BlockSpec((None, up_rows, w_up.shape[2]), lambda i: (layer, i, 0))],
        out_specs=(rows(d), rows(d), pl.BlockSpec((up_rows, w_up.shape[2]), lambda i: (i, 0))),
        scratch_shapes=[pltpu.VMEM((tm, d), BF16)],
        compiler_params=_cparams(("arbitrary",)),
        name="merge",
    )(a_act, o_gla, o_moba, gates, x, wc, wg, wm, wo, g_next, w_up)


def _shift_rows(u, k, carry):
    rolled = pltpu.roll(u, k, 0)
    head = jnp.where(lax.broadcasted_iota(jnp.int32, carry.shape, 0) < k,
                     pltpu.roll(carry, k, 0), rolled[:SUBLANE, :])
    return jnp.concatenate([head, rolled[SUBLANE:, :]], axis=0)


def _ffn_kernel(h_ref, wug_ref, wuv_ref, cwg_ref, cwv_ref, wd_ref, o_ref, cg, cv, *, tiles_per_seq):
    i = pl.program_id(0)
    j = pl.program_id(1)
    tm = h_ref.shape[0]

    @pl.when(j == 0)
    def _():
        o_ref[...] = jnp.zeros(o_ref.shape, o_ref.dtype)

    @pl.when(i % tiles_per_seq == 0)
    def _():
        cg[j] = jnp.zeros(cg.shape[1:], F32)
        cv[j] = jnp.zeros(cv.shape[1:], F32)

    h = h_ref[...]

    def conv(w_ref, cw_ref, carry_ref):
        u = jnp.dot(h, w_ref[...], preferred_element_type=F32)
        carry = carry_ref[j]
        out = (u * cw_ref[2:3, :] + _shift_rows(u, 1, carry) * cw_ref[1:2, :]
               + _shift_rows(u, 2, carry) * cw_ref[0:1, :])
        carry_ref[j] = u[tm - SUBLANE:, :]
        return out

    ug = conv(wug_ref, cwg_ref, cg)
    uv = conv(wuv_ref, cwv_ref, cv)
    act = (_silu(ug) * uv).astype(BF16)
    o_ref[...] += jnp.dot(act, wd_ref[...], preferred_element_type=F32)


def _ffn(h, w_up, conv_w, w_down, layer, *, seq, tm=1024, tf=512):
    m, d = h.shape
    f = w_down.shape[0]
    nf = f // tf
    return pl.pallas_call(
        functools.partial(_ffn_kernel, tiles_per_seq=seq // tm),
        out_shape=jax.ShapeDtypeStruct((m, d), F32),
        grid=(m // tm, nf),
        in_specs=[
            pl.BlockSpec((tm, d), lambda i, j: (i, 0)),
            pl.BlockSpec((d, tf), lambda i, j: (0, j)),
            pl.BlockSpec((d, tf), lambda i, j: (0, j + nf)),
            pl.BlockSpec((None, FFN_CONV_WIDTH, tf), lambda i, j: (layer, 0, j)),
            pl.BlockSpec((None, FFN_CONV_WIDTH, tf), lambda i, j: (layer, 0, j + nf)),
            pl.BlockSpec((tf, d), lambda i, j: (j, 0)),
        ],
        out_specs=pl.BlockSpec((tm, d), lambda i, j: (i, 0)),
        scratch_shapes=[
            pltpu.VMEM((nf, SUBLANE, tf), F32),
            pltpu.VMEM((nf, SUBLANE, tf), F32),
        ],
        compiler_params=_cparams(("arbitrary", "arbitrary")),
        name="conv_ffn",
    )(h, w_up, w_up, conv_w, conv_w, w_down)


PLE_COLS = 512


def _ple_kernel(x_ref, y_ref, p_ref, g_ref, wg_ref, wp_ref, *rest):
    h_scr = rest[-1]
    o_ref = rest[0] if len(rest) == 2 else rest[1]
    x2 = x_ref[...] + y_ref[...]
    o_ref[...] = x2
    h_scr[...] = _rms_rows(x2, g_ref[...]).astype(h_scr.dtype)
    p_bf = p_ref[...].astype(BF16)
    for c0 in range(0, D_MODEL, PLE_COLS):
        cs = slice(c0, c0 + PLE_COLS)
        gate = _sigmoid(jnp.dot(h_scr[...], wg_ref[:, cs], preferred_element_type=F32))
        emb = jnp.dot(p_bf, wp_ref[:, cs], preferred_element_type=F32)
        o_ref[:, cs] = o_ref[:, cs] + gate * emb
    if len(rest) == 4:
        gn_ref, _, hn_ref, _ = rest
        hn_ref[...] = _rms_rows(o_ref[...], gn_ref[...]).astype(hn_ref.dtype)


def _ple(x, y, p, g_ple, w_gate, w_ple, layer, g_next, *, tm=512):
    m, d = x.shape
    rows = lambda width: pl.BlockSpec((tm, width), lambda i: (i, 0))
    in_specs = [rows(d), rows(d), pl.BlockSpec((None, tm, PLE_DIM), lambda i: (layer, i, 0)),
                _layer_vec(d, layer), _resident(w_gate, layer), _resident(w_ple, layer)]
    args = [x, y, p, g_ple, w_gate, w_ple]
    out_shape = [jax.ShapeDtypeStruct((m, d), F32)]
    out_specs = [rows(d)]
    if g_next is not None:
        in_specs.append(_layer_vec(d, layer + 1))
        args.append(g_next)
        out_shape.append(jax.ShapeDtypeStruct((m, d), BF16))
        out_specs.append(rows(d))
    return pl.pallas_call(
        _ple_kernel,
        out_shape=tuple(out_shape),
        grid=(m // tm,),
        in_specs=in_specs,
        out_specs=tuple(out_specs),
        scratch_shapes=[pltpu.VMEM((tm, d), BF16)],
        compiler_params=_cparams(("arbitrary",)),
        name="ple",
    )(*args)


def kernel(x, p, norm_mix_g, w_in, conv_w, conv_b, conv_ln_g, conv_ln_b, w_conv_out, gla_w_a2, gla_b_a,
           gla_norm_g, w_gla_out, moba_q_norm_g, moba_k_norm_g, w_moba_out, w_gate, b_gate, w_out,
           norm_ffn_g, w_up, ffn_conv_w, w_down, norm_ple_g, w_ple_gate, w_ple):
    b_sz, s, d = x.shape
    depth = w_in.shape[0]
    m = b_sz * s
    xf = x.reshape(m, d)
    p2 = p.reshape(depth, m, PLE_DIM)
    row = lambda v: v.reshape(1, -1)
    stack_vec = lambda v: v.reshape(depth, 1, -1)
    w_in_t = jnp.swapaxes(w_in, 1, 2)
    w_ple_b = w_ple.astype(BF16)
    w_a2p = jnp.concatenate(
        [gla_w_a2, jnp.zeros((depth, GATE_PAD - GLA_GATE_RANK, GLA_QK), F32)], axis=1).astype(BF16)
    g_mix, g_ffn, g_ple, b_gate3 = (stack_vec(v) for v in (norm_mix_g, norm_ffn_g, norm_ple_g, b_gate))
    conv_b3, conv_ln_g3, conv_ln_b3 = (stack_vec(v) for v in (conv_b, conv_ln_g, conv_ln_b))

    h = _rmsnorm(xf, g_mix, 0)
    for i in range(depth):
        z3 = _proj_nt(h, w_in_t, i, 0, W_CONV_GLA, tm=2048, tn=512,
                      name="in_proj_conv_gla").reshape(b_sz, s, W_CONV_GLA)
        zg3 = _proj_nt(h, w_in_t, i, W_CONV_GLA, GATE_PAD, tm=2048, tn=GATE_PAD,
                       name="in_proj_gate").reshape(b_sz, s, GATE_PAD)
        zm3 = _proj_nt(h, w_in_t, i, W_CONV_GLA + GLA_GATE_RANK, W_MOBA, tm=2048, tn=1024,
                       name="in_proj_moba").reshape(b_sz, s, W_MOBA)
        gates = _proj_sigmoid(h, w_gate, b_gate3, i, tm=2048, tn=1024, name="gate_proj")
        a_act = _conformer_conv(z3, conv_w, conv_b3, conv_ln_g3, conv_ln_b3, i)
        o_gla, wc_b, wg_b, wm_b, wo_b, w_pg_b, w_down_b = _gla(
            z3, zg3, w_a2p[i], row(gla_b_a[i]), row(gla_norm_g[i]),
            (w_conv_out, w_gla_out, w_moba_out, w_out, w_ple_gate, w_down), i)
        o_moba = _moba(zm3, row(moba_q_norm_g[i]), row(moba_k_norm_g[i]))
        xf, h, w_up_b = _merge(a_act.reshape(m, -1), o_gla.reshape(m, -1), o_moba.reshape(m, -1),
                               gates, xf, wc_b, wg_b, wm_b, wo_b, g_ffn, w_up, i)
        y = _ffn(h, w_up_b, ffn_conv_w, w_down_b, i, seq=s)
        if i + 1 < depth:
            xf, h = _ple(xf, y, p2, g_ple, w_pg_b, w_ple_b, i, g_mix)
        else:
            (xf,) = _ple(xf, y, p2, g_ple, w_pg_b, w_ple_b, i, None)
    return xf.reshape(b_sz, s, d)
```

```python
import functools

import jax
import jax.numpy as jnp
from jax import lax
from jax.experimental import pallas as pl
from jax.experimental.pallas import tpu as pltpu

F32 = jnp.float32
BF16 = jnp.bfloat16

D_MODEL = 2048
PLE_DIM = 256
EPS = 1e-6
CONV_CH = 512
CONV_WIDTH = 31
GLA_HEADS = 4
GLA_DV = 128
GLA_DK = 64
GLA_GATE_RANK = 16
GLA_TAU = 16.0
GLA_CHUNK = 64
GLA_SUB = 16
MOBA_HEADS = 8
MOBA_HD = 128
MOBA_BLOCK = 256
MOBA_TOPK = 3
FFN_DIM = 5632
FFN_CONV_WIDTH = 3

LANE = 128
SUBLANE = 8
GATE_PAD = LANE
OFF_GLU = 0
OFF_GQ = OFF_GLU + 2 * CONV_CH
OFF_GK = OFF_GQ + GLA_HEADS * GLA_DK
OFF_GV = OFF_GK + GLA_HEADS * GLA_DK
OFF_GR = OFF_GV + GLA_HEADS * GLA_DV
W_CONV_GLA = OFF_GR + GLA_HEADS * GLA_DV
OFF_MQ = 0
OFF_MK = OFF_MQ + MOBA_HEADS * MOBA_HD
OFF_MV = OFF_MK + MOBA_HEADS * MOBA_HD
W_MOBA = OFF_MV + MOBA_HEADS * MOBA_HD
NT_DIMS = (((1,), (1,)), ((), ()))

VMEM_LIMIT = 56 * 1024 * 1024


def _cparams(sem):
    return pltpu.CompilerParams(dimension_semantics=sem, vmem_limit_bytes=VMEM_LIMIT)


def _sigmoid(x):
    return 1.0 / (1.0 + jnp.exp(-x))


def _silu(x):
    return x * _sigmoid(x)


def _rms_rows(x, g):
    ms = jnp.mean(x * x, axis=-1, keepdims=True)
    return x * lax.rsqrt(ms + EPS) * g


def _layer_vec(width, layer):
    return pl.BlockSpec((None, 1, width), lambda i: (layer, 0, 0))


def _rmsnorm_kernel(x_ref, g_ref, o_ref):
    o_ref[...] = _rms_rows(x_ref[...], g_ref[...]).astype(o_ref.dtype)


def _rmsnorm(x, g_stack, layer, *, tm=1024):
    m, d = x.shape
    return pl.pallas_call(
        _rmsnorm_kernel,
        out_shape=jax.ShapeDtypeStruct((m, d), BF16),
        grid=(m // tm,),
        in_specs=[pl.BlockSpec((tm, d), lambda i: (i, 0)), _layer_vec(d, layer)],
        out_specs=pl.BlockSpec((tm, d), lambda i: (i, 0)),
        compiler_params=_cparams(("arbitrary",)),
        name="rmsnorm",
    )(x, g_stack)


def _proj_nt_kernel(h_ref, wt_ref, o_ref):
    acc = lax.dot_general(h_ref[...], wt_ref[0].astype(BF16), NT_DIMS, preferred_element_type=F32)
    o_ref[...] = acc.astype(o_ref.dtype)


def _proj_nt(h, wt_stack, layer, row0, n, *, tm, tn, name):
    m, d = h.shape
    return pl.pallas_call(
        _proj_nt_kernel,
        out_shape=jax.ShapeDtypeStruct((m, n), BF16),
        grid=(m // tm, n // tn),
        in_specs=[pl.BlockSpec((tm, d), lambda i, j: (i, 0)),
                  pl.BlockSpec((pl.Element(1), pl.Element(tn), pl.Element(d)),
                               lambda i, j: (layer, pl.multiple_of(row0 + j * tn, SUBLANE), 0))],
        out_specs=pl.BlockSpec((tm, tn), lambda i, j: (i, j)),
        compiler_params=_cparams(("arbitrary", "arbitrary")),
        name=name,
    )(h, wt_stack)


def _proj_bias_kernel(h_ref, w_ref, b_ref, o_ref):
    acc = jnp.dot(h_ref[...], w_ref[...].astype(BF16), preferred_element_type=F32)
    o_ref[...] = (acc + b_ref[...]).astype(o_ref.dtype)


def _proj_bias(h, w_stack, b_stack, layer, *, tm, tn, name):
    m, d = h.shape
    n = w_stack.shape[2]
    return pl.pallas_call(
        _proj_bias_kernel,
        out_shape=jax.ShapeDtypeStruct((m, n), BF16),
        grid=(m // tm, n // tn),
        in_specs=[pl.BlockSpec((tm, d), lambda i, j: (i, 0)),
                  pl.BlockSpec((None, d, tn), lambda i, j: (layer, 0, j)),
                  pl.BlockSpec((None, 1, tn), lambda i, j: (layer, 0, j))],
        out_specs=pl.BlockSpec((tm, tn), lambda i, j: (i, j)),
        compiler_params=_cparams(("arbitrary", "arbitrary")),
        name=name,
    )(h, w_stack, b_stack)


CONV_HALO = 32
CONV_ROWS = 64


def _conv_kernel(z_ref, w_ref, cb_ref, lg_ref, lb_ref, o_ref, abuf, sh, *, ts):
    t = pl.program_id(1)

    @pl.when(t == 0)
    def _():
        abuf[0:CONV_HALO, :] = jnp.zeros((CONV_HALO, CONV_CH), F32)

    @pl.when(t > 0)
    def _():
        abuf[0:CONV_HALO, :] = abuf[ts:ts + CONV_HALO, :]

    z = z_ref[...].astype(F32)
    abuf[CONV_HALO:CONV_HALO + ts, :] = z[:, :CONV_CH] * _sigmoid(z[:, CONV_CH:])

    sh_rows = sh.shape[1]
    for b in range(1, SUBLANE):
        sh[b] = abuf[b:b + sh_rows, :]

    first = CONV_HALO - (CONV_WIDTH - 1)
    for r0 in range(0, ts, CONV_ROWS):
        acc = abuf[r0 + CONV_HALO:r0 + CONV_HALO + CONV_ROWS, :] * w_ref[CONV_WIDTH - 1:CONV_WIDTH, :]
        for j in range(CONV_WIDTH - 1):
            res = (first + j) % SUBLANE
            base = r0 + first + j - res
            rows = abuf[base:base + CONV_ROWS, :] if res == 0 else sh[res, base:base + CONV_ROWS, :]
            acc = acc + rows * w_ref[j:j + 1, :]
        acc = acc + cb_ref[...]
        mu = jnp.mean(acc, axis=-1, keepdims=True)
        xc = acc - mu
        var = jnp.mean(xc * xc, axis=-1, keepdims=True)
        y = xc * lax.rsqrt(var + EPS) * lg_ref[...] + lb_ref[...]
        o_ref[r0:r0 + CONV_ROWS, :] = _silu(y).astype(o_ref.dtype)


def _conformer_conv(z3, conv_w, conv_b, ln_g, ln_b, layer, *, ts=512):
    b_sz, s, _ = z3.shape

    def par(a):
        return pl.BlockSpec((None,) + a.shape[1:], lambda b, t: (layer, 0, 0))

    return pl.pallas_call(
        functools.partial(_conv_kernel, ts=ts),
        out_shape=jax.ShapeDtypeStruct((b_sz, s, CONV_CH), BF16),
        grid=(b_sz, s // ts),
        in_specs=[
            pl.BlockSpec((None, ts, 2 * CONV_CH), lambda b, t: (b, t, OFF_GLU // (2 * CONV_CH))),
            par(conv_w), par(conv_b), par(ln_g), par(ln_b),
        ],
        out_specs=pl.BlockSpec((None, ts, CONV_CH), lambda b, t: (b, t, 0)),
        scratch_shapes=[pltpu.VMEM((CONV_HALO + ts, CONV_CH), F32),
                        pltpu.VMEM((SUBLANE, CONV_HALO + ts - SUBLANE, CONV_CH), F32)],
        compiler_params=_cparams(("arbitrary", "arbitrary")),
        name="conformer_conv",
    )(z3, conv_w, conv_b, ln_g, ln_b)


LOG2_E = 1.4426950408889634
GLA_QK = GLA_HEADS * GLA_DK
GLA_V = GLA_HEADS * GLA_DV


GLA_CPS = 4


def _gla_kernel(*refs, n_narrow):
    q_ref, k_ref, v_ref, r_ref, al_ref, wa_ref, ba_ref, ng_ref = refs[:8]
    w_f32 = refs[8:8 + n_narrow]
    o_ref = refs[8 + n_narrow]
    w_bf16 = refs[9 + n_narrow:9 + 2 * n_narrow]
    st, qf, kf, vf, bcs, pbig, intra = refs[9 + 2 * n_narrow:]

    for src, dst in zip(w_f32, w_bf16):
        dst[...] = src[...].astype(dst.dtype)

    @pl.when(pl.program_id(1) == 0)
    def _():
        st[...] = jnp.zeros(st.shape, F32)

    for ci in range(GLA_CPS):
        rs = slice(ci * GLA_CHUNK, (ci + 1) * GLA_CHUNK)
        _gla_chunk(q_ref.at[rs, :], k_ref.at[rs, :], v_ref.at[rs, :], r_ref.at[rs, :], al_ref.at[rs, :],
                   wa_ref, ba_ref, ng_ref, o_ref.at[rs, :],
                   st, qf.at[ci], kf.at[ci], vf.at[ci], bcs.at[ci], pbig.at[ci], intra.at[ci])


def _gla_chunk(q_ref, k_ref, v_ref, r_ref, al_ref, wa_ref, ba_ref, ng_ref, o_ref,
               st, qf, kf, vf, bcs, pbig, intra):
    c = GLA_CHUNK

    xa = jnp.dot(al_ref[...], wa_ref[...], preferred_element_type=F32) + ba_ref[...]
    la = (jnp.minimum(xa, 0.0) - jnp.log(1.0 + jnp.exp(-jnp.abs(xa)))) * (1.0 / GLA_TAU)
    row = lax.broadcasted_iota(jnp.int32, (c, c), 0)
    col = lax.broadcasted_iota(jnp.int32, (c, c), 1)
    tri = jnp.where(row >= col, 1.0, 0.0).astype(BF16)
    la_hi = la.astype(BF16)
    la_lo = (la - la_hi.astype(F32)).astype(BF16)
    bc = (jnp.dot(tri, la_hi, preferred_element_type=F32)
          + jnp.dot(tri, la_lo, preferred_element_type=F32))

    q = q_ref[...].astype(F32) * (GLA_DK ** -0.5)
    k = k_ref[...].astype(F32)
    qf[...] = q
    kf[...] = k
    vf[...] = v_ref[...].astype(F32)
    bcs[...] = bc * LOG2_E

    b_last = bc[c - 1:c, :]
    q_in = (q * jnp.exp(bc)).astype(BF16)
    k_dec = (k * jnp.exp(b_last - bc)).astype(BF16)
    e_last = jnp.exp(b_last)
    v_bf = v_ref[...]

    srow = lax.broadcasted_iota(jnp.int32, (GLA_QK, GLA_V), 0) // GLA_DK
    scol = lax.broadcasted_iota(jnp.int32, (GLA_QK, GLA_V), 1) // GLA_DV
    sel = jnp.where(srow == scol, 1.0, 0.0).astype(BF16)

    for j in range(c // GLA_SUB):
        t0 = j * GLA_SUB
        tj = c - t0
        q_rows = qf[t0:c, :]
        bc_rows = bcs[t0:c, :]
        t_loc = lax.broadcasted_iota(jnp.int32, (GLA_SUB, GLA_QK), 0)
        for sl in range(GLA_SUB):
            s = t0 + sl
            d = bc_rows - bcs[s:s + 1, :]
            d_own = jnp.minimum(d[:GLA_SUB, :], 0.0)
            d = d_own if tj == GLA_SUB else jnp.concatenate([d_own, d[GLA_SUB:, :]], axis=0)
            p = (q_rows * kf[s:s + 1, :]) * jnp.exp2(d)
            p_own = jnp.where(t_loc >= sl, p[:GLA_SUB, :], 0.0)
            p = p_own if tj == GLA_SUB else jnp.concatenate([p_own, p[GLA_SUB:, :]], axis=0)
            pbig[sl * tj:(sl + 1) * tj, :] = p.astype(BF16)
        r_all = jnp.dot(pbig[0:GLA_SUB * tj, :], sel, preferred_element_type=F32)
        acc = jnp.zeros((tj, GLA_V), F32)
        for sl in range(GLA_SUB):
            s = t0 + sl
            acc = acc + r_all[sl * tj:(sl + 1) * tj, :] * vf[s:s + 1, :]
        if j == 0:
            intra[...] = acc
        else:
            intra[t0:c, :] = intra[t0:c, :] + acc

    for h in range(GLA_HEADS):
        ks = slice(h * GLA_DK, (h + 1) * GLA_DK)
        vs = slice(h * GLA_DV, (h + 1) * GLA_DV)
        st_h = st[h]
        inter = lax.dot_general(q_in[:, ks], st_h.astype(BF16), (((1,), (1,)), ((), ())),
                                preferred_element_type=F32)
        upd = lax.dot_general(v_bf[:, vs], k_dec[:, ks], (((0,), (0,)), ((), ())),
                              preferred_element_type=F32)
        st[h] = st_h * e_last[:, ks] + upd
        o_h = inter + intra[:, vs]
        y = _rms_rows(o_h, ng_ref[...])
        o_ref[:, vs] = (y * _silu(r_ref[:, vs].astype(F32))).astype(o_ref.dtype)


def _gla(z3, zg3, w_a2p, b_a, norm_g, narrow, layer):
    b_sz, s, _ = z3.shape
    c = GLA_CHUNK
    rows = c * GLA_CPS
    cps = GLA_CPS
    t_steps = s // rows
    n_steps = b_sz * t_steps

    def zspec(width, off):
        return pl.BlockSpec((None, rows, width), lambda b, t: (b, t, off // width))

    slab = [w.shape[1] // n_steps for w in narrow]
    return pl.pallas_call(
        functools.partial(_gla_kernel, n_narrow=len(narrow)),
        out_shape=(jax.ShapeDtypeStruct((b_sz, s, GLA_V), BF16),
                   *[jax.ShapeDtypeStruct(w.shape[1:], BF16) for w in narrow]),
        grid=(b_sz, t_steps),
        in_specs=[
            zspec(GLA_QK, OFF_GQ), zspec(GLA_QK, OFF_GK), zspec(GLA_V, OFF_GV), zspec(GLA_V, OFF_GR),
            zspec(GATE_PAD, 0),
            pl.BlockSpec((GATE_PAD, GLA_QK), lambda b, t: (0, 0)),
            pl.BlockSpec((1, GLA_QK), lambda b, t: (0, 0)),
            pl.BlockSpec((1, GLA_DV), lambda b, t: (0, 0)),
            *[pl.BlockSpec((None, r, w.shape[2]), lambda b, t: (layer, b * t_steps + t, 0))
              for r, w in zip(slab, narrow)],
        ],
        out_specs=(pl.BlockSpec((None, rows, GLA_V), lambda b, t: (b, t, 0)),
                   *[pl.BlockSpec((r, w.shape[2]), lambda b, t: (b * t_steps + t, 0))
                     for r, w in zip(slab, narrow)]),
        scratch_shapes=[
            pltpu.VMEM((GLA_HEADS, GLA_DV, GLA_DK), F32),
            pltpu.VMEM((cps, c, GLA_QK), F32),
            pltpu.VMEM((cps, c, GLA_QK), F32),
            pltpu.VMEM((cps, c, GLA_V), F32),
            pltpu.VMEM((cps, c, GLA_QK), F32),
            pltpu.VMEM((cps, GLA_SUB * c, GLA_QK), BF16),
            pltpu.VMEM((cps, c, GLA_V), F32),
        ],
        compiler_params=_cparams(("arbitrary", "arbitrary")),
        name="gla",
    )(z3, z3, z3, z3, zg3, w_a2p, b_a, norm_g, *narrow)


NEG_INF = float("-inf")
MOBA_GROUP = 4


def _fold_rows(x, op):
    return op(x.reshape(x.shape[0] // SUBLANE, SUBLANE, x.shape[1]), axis=0)


MOBA_HPS = 4


def _moba_kernel(*refs, n_blk):
    hps = MOBA_HPS
    q_refs, k_refs, v_refs = refs[:hps], refs[hps:2 * hps], refs[2 * hps:3 * hps]
    gq_ref, gk_ref, o_ref, kn, vt, kmean, chosen, s_scr, s_own, m_acc, l_acc, o_acc = refs[3 * hps:]
    qi = pl.program_id(2)
    blk = MOBA_BLOCK
    grp = MOBA_GROUP
    heads = range(hps)

    @pl.when(qi == 0)
    def _():
        for hh in heads:
            for n in range(n_blk):
                kb = _rms_rows(k_refs[hh][n * blk:(n + 1) * blk, :].astype(F32), gk_ref[...])
                kn[hh, n * blk:(n + 1) * blk, :] = kb.astype(BF16)
                kmean[hh, n:n + 1, :] = jnp.mean(kb, axis=0, keepdims=True)
                vt[hh, n] = v_refs[hh][n * blk:(n + 1) * blk, :].astype(F32).T.astype(BF16)

    brow = lax.broadcasted_iota(jnp.int32, (n_blk, blk), 0)
    past = brow < qi
    k_i = lax.broadcasted_iota(jnp.int32, (blk, blk), 0)
    q_i = lax.broadcasted_iota(jnp.int32, (blk, blk), 1)
    own = pl.multiple_of(qi * blk, blk)
    q_bf = []
    for hh in heads:
        qn = _rms_rows(q_refs[hh][...].astype(F32), gq_ref[...])
        q_bf.append((qn * (MOBA_HD ** -0.5 * LOG2_E)).astype(BF16))
        sb = lax.dot_general(kmean[hh], qn, NT_DIMS, precision=lax.Precision.HIGHEST,
                             preferred_element_type=F32)
        sb = jnp.where(past, sb, NEG_INF)
        rank = jnp.zeros((n_blk, blk), F32)
        for m in range(n_blk):
            cm = sb[m:m + 1, :]
            ge = jnp.where(cm >= sb, 1.0, 0.0)
            gt = jnp.where(cm > sb, 1.0, 0.0)
            rank = rank + jnp.where(brow > m, ge, gt)
        chosen[hh] = jnp.where(past, jnp.where(rank < float(MOBA_TOPK), 1.0, 0.0), 0.0)
        so = lax.dot_general(kn[hh, pl.ds(own, blk), :], q_bf[hh], NT_DIMS, preferred_element_type=F32)
        so = jnp.where(k_i <= q_i, so, NEG_INF)
        s_own[hh] = so
        m_acc[hh] = _fold_rows(so, jnp.max)

    def for_past_blocks(visit):
        n_full = qi // grp

        def body(g, c):
            visit(g * grp, grp)
            return c

        lax.fori_loop(0, n_full, body, 0)
        for rem in range(1, grp):
            @pl.when(qi % grp == rem)
            def _():
                visit(n_full * grp, rem)

    def scores(first, count):
        start = pl.multiple_of(first * blk, blk)
        for hh in heads:
            sg = lax.dot_general(kn[hh, pl.ds(start, count * blk), :], q_bf[hh], NT_DIMS,
                                 preferred_element_type=F32)
            m_run = m_acc[hh]
            for b in range(count):
                sn = jnp.where(chosen[hh, pl.ds(first + b, 1), :] > 0.0, sg[b * blk:(b + 1) * blk, :], NEG_INF)
                s_scr[hh, first + b] = sn
                m_run = jnp.maximum(m_run, _fold_rows(sn, jnp.max))
            m_acc[hh] = m_run

    for_past_blocks(scores)
    m_rows = [jnp.max(m_acc[hh], axis=0, keepdims=True) for hh in heads]

    for hh in heads:
        p_own = jnp.exp2(s_own[hh] - m_rows[hh])
        l_acc[hh] = _fold_rows(p_own, jnp.sum)
        o_acc[hh] = jnp.dot(vt[hh, qi], p_own.astype(BF16), preferred_element_type=F32)

    def values(first, count):
        for hh in heads:
            l_run = l_acc[hh]
            ps, vs = [], []
            for b in range(count):
                pn = jnp.exp2(s_scr[hh, first + b] - m_rows[hh])
                l_run = l_run + _fold_rows(pn, jnp.sum)
                ps.append(pn.astype(BF16))
                vs.append(vt[hh, first + b])
            l_acc[hh] = l_run
            o_acc[hh] += jnp.dot(jnp.concatenate(vs, axis=1), jnp.concatenate(ps, axis=0),
                                 preferred_element_type=F32)

    for_past_blocks(values)
    for hh in heads:
        l_row = jnp.sum(l_acc[hh], axis=0, keepdims=True)
        o_ref[:, hh * MOBA_HD:(hh + 1) * MOBA_HD] = (o_acc[hh] / l_row).T.astype(o_ref.dtype)


def _moba(z3, gq, gk):
    b_sz, s, _ = z3.shape
    n_blk = s // MOBA_BLOCK
    hd = MOBA_HD
    hps = MOBA_HPS

    def head_spec(rows, off, hh, row_index):
        return pl.BlockSpec((None, rows, hd), lambda b, h, i: (b, row_index(i), off // hd + h * hps + hh))

    q_specs = [head_spec(MOBA_BLOCK, OFF_MQ, hh, lambda i: i) for hh in range(hps)]
    k_specs = [head_spec(s, OFF_MK, hh, lambda i: 0) for hh in range(hps)]
    v_specs = [head_spec(s, OFF_MV, hh, lambda i: 0) for hh in range(hps)]
    vec = pl.BlockSpec((1, hd), lambda b, h, i: (0, 0))
    return pl.pallas_call(
        functools.partial(_moba_kernel, n_blk=n_blk),
        out_shape=jax.ShapeDtypeStruct((b_sz, s, MOBA_HEADS * hd), BF16),
        grid=(b_sz, MOBA_HEADS // hps, n_blk),
        in_specs=q_specs + k_specs + v_specs + [vec, vec],
        out_specs=pl.BlockSpec((None, MOBA_BLOCK, hps * hd), lambda b, h, i: (b, i, h)),
        scratch_shapes=[
            pltpu.VMEM((hps, s, hd), BF16),
            pltpu.VMEM((hps, n_blk, hd, MOBA_BLOCK), BF16),
            pltpu.VMEM((hps, n_blk, hd), F32),
            pltpu.VMEM((hps, n_blk, MOBA_BLOCK), F32),
            pltpu.VMEM((hps, n_blk, MOBA_BLOCK, MOBA_BLOCK), F32),
            pltpu.VMEM((hps, MOBA_BLOCK, MOBA_BLOCK), F32),
            pltpu.VMEM((hps, SUBLANE, MOBA_BLOCK), F32),
            pltpu.VMEM((hps, SUBLANE, MOBA_BLOCK), F32),
            pltpu.VMEM((hps, hd, MOBA_BLOCK), F32),
        ],
        compiler_params=_cparams(("arbitrary", "arbitrary", "arbitrary")),
        name="moba",
    )(*([z3] * (3 * hps)), gq, gk)


MERGE_COLS = 512


def _merge_kernel(a_ref, gl_ref, mo_ref, gt_ref, x_ref, wc_ref, wg_ref, wm_ref, wo_ref, gn_ref,
                  wup_ref, o_ref, hn_ref, wup_o, mg):
    d = D_MODEL
    wup_o[...] = wup_ref[...].astype(wup_o.dtype)
    for c0 in range(0, d, MERGE_COLS):
        cs = slice(c0, c0 + MERGE_COLS)
        ya = jnp.dot(a_ref[...], wc_ref[:, cs], preferred_element_type=F32)
        yb = jnp.dot(gl_ref[...], wg_ref[:, cs], preferred_element_type=F32)
        yc = jnp.dot(mo_ref[...], wm_ref[:, cs], preferred_element_type=F32)
        g0 = _sigmoid(gt_ref[:, c0:c0 + MERGE_COLS].astype(F32))
        g1 = _sigmoid(gt_ref[:, d + c0:d + c0 + MERGE_COLS].astype(F32))
        g2 = _sigmoid(gt_ref[:, 2 * d + c0:2 * d + c0 + MERGE_COLS].astype(F32))
        mg[:, cs] = (g0 * ya + g1 * yb + g2 * yc).astype(BF16)
    x_new = x_ref[...] + jnp.dot(mg[...], wo_ref[...], preferred_element_type=F32)
    o_ref[...] = x_new
    hn_ref[...] = _rms_rows(x_new, gn_ref[...]).astype(hn_ref.dtype)


def _resident(w, layer):
    if w.ndim == 2:
        return pl.BlockSpec(w.shape, lambda i: (0, 0), pipeline_mode=pl.Buffered(1))
    return pl.BlockSpec((None,) + w.shape[1:], lambda i: (layer, 0, 0), pipeline_mode=pl.Buffered(1))


def _merge(a_act, o_gla, o_moba, gates, x, wc, wg, wm, wo, g_next, w_up, layer, *, tm=256):
    m, d = x.shape
    steps = m // tm
    up_rows = w_up.shape[1] // steps

    def rows(width):
        return pl.BlockSpec((tm, width), lambda i: (i, 0))

    return pl.pallas_call(
        _merge_kernel,
        out_shape=(jax.ShapeDtypeStruct((m, d), F32), jax.ShapeDtypeStruct((m, d), BF16),
                   jax.ShapeDtypeStruct(w_up.shape[1:], BF16)),
        grid=(steps,),
        in_specs=[rows(a_act.shape[1]), rows(o_gla.shape[1]), rows(o_moba.shape[1]), rows(gates.shape[1]),
                  rows(d), _resident(wc, layer), _resident(wg, layer), _resident(wm, layer),
                  _resident(wo, layer), _layer_vec(d, layer),
                  pl.BlockSpec((None, up_rows, w_up.shape[2]), lambda i: (layer, i, 0))],
        out_specs=(rows(d), rows(d), pl.BlockSpec((up_rows, w_up.shape[2]), lambda i: (i, 0))),
        scratch_shapes=[pltpu.VMEM((tm, d), BF16)],
        compiler_params=_cparams(("arbitrary",)),
        name="merge",
    )(a_act, o_gla, o_moba, gates, x, wc, wg, wm, wo, g_next, w_up)


def _shift_rows(u, k, carry):
    rolled = pltpu.roll(u, k, 0)
    head = jnp.where(lax.broadcasted_iota(jnp.int32, carry.shape, 0) < k,
                     pltpu.roll(carry, k, 0), rolled[:SUBLANE, :])
    return jnp.concatenate([head, rolled[SUBLANE:, :]], axis=0)


def _ffn_kernel(h_ref, wug_ref, wuv_ref, cwg_ref, cwv_ref, wd_ref, o_ref, cg, cv, *, tiles_per_seq):
    i = pl.program_id(0)
    j = pl.program_id(1)
    tm = h_ref.shape[0]

    @pl.when(j == 0)
    def _():
        o_ref[...] = jnp.zeros(o_ref.shape, o_ref.dtype)

    @pl.when(i % tiles_per_seq == 0)
    def _():
        cg[j] = jnp.zeros(cg.shape[1:], F32)
        cv[j] = jnp.zeros(cv.shape[1:], F32)

    h = h_ref[...]

    def conv(w_ref, cw_ref, carry_ref):
        u = jnp.dot(h, w_ref[...], preferred_element_type=F32)
        carry = carry_ref[j]
        out = (u * cw_ref[2:3, :] + _shift_rows(u, 1, carry) * cw_ref[1:2, :]
               + _shift_rows(u, 2, carry) * cw_ref[0:1, :])
        carry_ref[j] = u[tm - SUBLANE:, :]
        return out

    ug = conv(wug_ref, cwg_ref, cg)
    uv = conv(wuv_ref, cwv_ref, cv)
    act = (_silu(ug) * uv).astype(BF16)
    o_ref[...] += jnp.dot(act, wd_ref[...], preferred_element_type=F32)


def _ffn(h, w_up, conv_w, w_down, layer, *, seq, tm=1024, tf=512):
    m, d = h.shape
    f = w_down.shape[0]
    nf = f // tf
    return pl.pallas_call(
        functools.partial(_ffn_kernel, tiles_per_seq=seq // tm),
        out_shape=jax.ShapeDtypeStruct((m, d), F32),
        grid=(m // tm, nf),
        in_specs=[
            pl.BlockSpec((tm, d), lambda i, j: (i, 0)),
            pl.BlockSpec((d, tf), lambda i, j: (0, j)),
            pl.BlockSpec((d, tf), lambda i, j: (0, j + nf)),
            pl.BlockSpec((None, FFN_CONV_WIDTH, tf), lambda i, j: (layer, 0, j)),
            pl.BlockSpec((None, FFN_CONV_WIDTH, tf), lambda i, j: (layer, 0, j + nf)),
            pl.BlockSpec((tf, d), lambda i, j: (j, 0)),
        ],
        out_specs=pl.BlockSpec((tm, d), lambda i, j: (i, 0)),
        scratch_shapes=[
            pltpu.VMEM((nf, SUBLANE, tf), F32),
            pltpu.VMEM((nf, SUBLANE, tf), F32),
        ],
        compiler_params=_cparams(("arbitrary", "arbitrary")),
        name="conv_ffn",
    )(h, w_up, w_up, conv_w, conv_w, w_down)


PLE_COLS = 512


def _ple_kernel(x_ref, y_ref, p_ref, g_ref, wg_ref, wp_ref, *rest):
    h_scr = rest[-1]
    o_ref = rest[0] if len(rest) == 2 else rest[1]
    x2 = x_ref[...] + y_ref[...]
    o_ref[...] = x2
    h_scr[...] = _rms_rows(x2, g_ref[...]).astype(h_scr.dtype)
    p_bf = p_ref[...].astype(BF16)
    for c0 in range(0, D_MODEL, PLE_COLS):
        cs = slice(c0, c0 + PLE_COLS)
        gate = _sigmoid(jnp.dot(h_scr[...], wg_ref[:, cs], preferred_element_type=F32))
        emb = jnp.dot(p_bf, wp_ref[:, cs], preferred_element_type=F32)
        o_ref[:, cs] = o_ref[:, cs] + gate * emb
    if len(rest) == 4:
        gn_ref, _, hn_ref, _ = rest
        hn_ref[...] = _rms_rows(o_ref[...], gn_ref[...]).astype(hn_ref.dtype)


def _ple(x, y, p, g_ple, w_gate, w_ple, layer, g_next, *, tm=512):
    m, d = x.shape
    rows = lambda width: pl.BlockSpec((tm, width), lambda i: (i, 0))
    in_specs = [rows(d), rows(d), pl.BlockSpec((None, tm, PLE_DIM), lambda i: (layer, i, 0)),
                _layer_vec(d, layer), _resident(w_gate, layer), _resident(w_ple, layer)]
    args = [x, y, p, g_ple, w_gate, w_ple]
    out_shape = [jax.ShapeDtypeStruct((m, d), F32)]
    out_specs = [rows(d)]
    if g_next is not None:
        in_specs.append(_layer_vec(d, layer + 1))
        args.append(g_next)
        out_shape.append(jax.ShapeDtypeStruct((m, d), BF16))
        out_specs.append(rows(d))
    return pl.pallas_call(
        _ple_kernel,
        out_shape=tuple(out_shape),
        grid=(m // tm,),
        in_specs=in_specs,
        out_specs=tuple(out_specs),
        scratch_shapes=[pltpu.VMEM((tm, d), BF16)],
        compiler_params=_cparams(("arbitrary",)),
        name="ple",
    )(*args)


def kernel(x, p, norm_mix_g, w_in, conv_w, conv_b, conv_ln_g, conv_ln_b, w_conv_out, gla_w_a2, gla_b_a,
           gla_norm_g, w_gla_out, moba_q_norm_g, moba_k_norm_g, w_moba_out, w_gate, b_gate, w_out,
           norm_ffn_g, w_up, ffn_conv_w, w_down, norm_ple_g, w_ple_gate, w_ple):
    b_sz, s, d = x.shape
    depth = w_in.shape[0]
    m = b_sz * s
    xf = x.reshape(m, d)
    p2 = p.reshape(depth, m, PLE_DIM)
    row = lambda v: v.reshape(1, -1)
    stack_vec = lambda v: v.reshape(depth, 1, -1)
    w_in_t = jnp.swapaxes(w_in, 1, 2)
    w_ple_b = w_ple.astype(BF16)
    w_a2p = jnp.concatenate(
        [gla_w_a2, jnp.zeros((depth, GATE_PAD - GLA_GATE_RANK, GLA_QK), F32)], axis=1).astype(BF16)
    g_mix, g_ffn, g_ple, b_gate3 = (stack_vec(v) for v in (norm_mix_g, norm_ffn_g, norm_ple_g, b_gate))
    conv_b3, conv_ln_g3, conv_ln_b3 = (stack_vec(v) for v in (conv_b, conv_ln_g, conv_ln_b))

    h = _rmsnorm(xf, g_mix, 0)
    for i in range(depth):
        z3 = _proj_nt(h, w_in_t, i, 0, W_CONV_GLA, tm=2048, tn=512,
                      name="in_proj_conv_gla").reshape(b_sz, s, W_CONV_GLA)
        zg3 = _proj_nt(h, w_in_t, i, W_CONV_GLA, GATE_PAD, tm=2048, tn=GATE_PAD,
                       name="in_proj_gate").reshape(b_sz, s, GATE_PAD)
        zm3 = _proj_nt(h, w_in_t, i, W_CONV_GLA + GLA_GATE_RANK, W_MOBA, tm=2048, tn=1024,
                       name="in_proj_moba").reshape(b_sz, s, W_MOBA)
        gates = _proj_bias(h, w_gate, b_gate3, i, tm=2048, tn=1024, name="gate_proj")
        a_act = _conformer_conv(z3, conv_w, conv_b3, conv_ln_g3, conv_ln_b3, i)
        o_gla, wc_b, wg_b, wm_b, wo_b, w_pg_b, w_down_b = _gla(
            z3, zg3, w_a2p[i], row(gla_b_a[i]), row(gla_norm_g[i]),
            (w_conv_out, w_gla_out, w_moba_out, w_out, w_ple_gate, w_down), i)
        o_moba = _moba(zm3, row(moba_q_norm_g[i]), row(moba_k_norm_g[i]))
        xf, h, w_up_b = _merge(a_act.reshape(m, -1), o_gla.reshape(m, -1), o_moba.reshape(m, -1),
                               gates, xf, wc_b, wg_b, wm_b, wo_b, g_ffn, w_up, i)
        y = _ffn(h, w_up_b, ffn_conv_w, w_down_b, i, seq=s)
        if i + 1 < depth:
            xf, h = _ple(xf, y, p2, g_ple, w_pg_b, w_ple_b, i, g_mix)
        else:
            (xf,) = _ple(xf, y, p2, g_ple, w_pg_b, w_ple_b, i, None)
    return xf.reshape(b_sz, s, d)
```
